```python
import math
import jax, jax.numpy as jnp
from jax import lax
import numpy as np

D_MODEL = 1024
BATCH = 8
SEQ = 2048
DEPTH = 2

EPS = 1e-6
D_FF = 2816
Q_BLOCK = 128
NUM_BUCKETS = 32
MAX_DISTANCE = 1024
ROPE_THETA = 10000.0
NEG_BIG = -1e30

DIFF_HEADS = 4
DIFF_QK_DIM = 32
DIFF_V_DIM = 64
DIL_HEADS = 6
DIL_HEAD_DIM = 64
DIL_PATTERNS = ((128, 1), (512, 4), (2048, 16))
MLA_HEADS = 6
MLA_Q_RANK = 256
MLA_KV_RANK = 128
MLA_NOPE_DIM = 64
MLA_ROPE_DIM = 32
MLA_V_DIM = 64

DIFF_WIDTH = DIFF_HEADS * DIFF_V_DIM
DIL_WIDTH = DIL_HEADS * DIL_HEAD_DIM
MLA_WIDTH = MLA_HEADS * MLA_V_DIM
MIX_WIDTH = DIFF_WIDTH + DIL_WIDTH + MLA_WIDTH
BIAS_HEADS = DIFF_HEADS + DIL_HEADS
IN_SIZES = (DIFF_HEADS * 2 * DIFF_QK_DIM, DIFF_HEADS * 2 * DIFF_QK_DIM, DIFF_WIDTH,
            DIL_WIDTH, DIL_WIDTH, DIL_WIDTH,
            MLA_Q_RANK, MLA_KV_RANK + MLA_ROPE_DIM)
IN_WIDTH = sum(IN_SIZES)

kernel_name = 'hybrid_parallel_diff_dilated_mla_encoder'


def rms_norm(x, g):
    xf = x.astype(jnp.float32)
    y = xf * lax.rsqrt(jnp.mean(xf * xf, axis=-1, keepdims=True) + EPS)
    return (y * g.astype(jnp.float32)).astype(x.dtype)


def swiglu(x, wg, wu, wd):
    return (jax.nn.silu(x @ wg) * (x @ wu)) @ wd


def t5_bucket(rel):
    half = NUM_BUCKETS // 2
    max_exact = half // 2
    n = jnp.abs(rel)
    nf = jnp.maximum(n, 1).astype(jnp.float32)
    large = max_exact + (jnp.log(nf / max_exact) / math.log(MAX_DISTANCE / max_exact)
                         * (half - max_exact)).astype(jnp.int32)
    large = jnp.minimum(large, half - 1)
    return jnp.where(rel > 0, half, 0) + jnp.where(n < max_exact, n, large)


def rope(x, pos):
    half = x.shape[-1] // 2
    inv = ROPE_THETA ** (-jnp.arange(half, dtype=jnp.float32) / half)
    ang = pos.astype(jnp.float32)[:, None] * inv[None, :]
    cos, sin = jnp.cos(ang), jnp.sin(ang)
    x1, x2 = x[..., :half], x[..., half:]
    return jnp.concatenate([x1 * cos - x2 * sin, x1 * sin + x2 * cos], axis=-1).astype(x.dtype)


def diff_attention(q, k, v, lam, lambda_init, subln_g, bias_table):
    B, H, S, _, dk = q.shape
    dv = v.shape[-1]
    nblk = S // Q_BLOCK
    qb = q.reshape(B, H, nblk, Q_BLOCK, 2, dk).transpose(2, 0, 1, 3, 4, 5)
    kpos = jnp.arange(S)
    scale = dk ** -0.5
    table = bias_table.astype(jnp.float32)

    def block(args):
        qi, start = args
        rel = kpos[None, :] - (start + jnp.arange(Q_BLOCK))[:, None]
        bias = jnp.moveaxis(table[t5_bucket(rel)], -1, 0)
        logits = jnp.einsum('bhqcd,bhkcd->cbhqk', qi, k).astype(jnp.float32) * scale + bias
        p = jax.nn.softmax(logits, axis=-1)
        a = p[0] - lam * p[1]
        return jnp.einsum('bhqk,bhkd->bhqd', a.astype(v.dtype), v)

    starts = jnp.arange(nblk, dtype=jnp.int32) * Q_BLOCK
    out = lax.map(block, (qb, starts))
    out = out.transpose(1, 2, 0, 3, 4).reshape(B, H, S, dv)
    return rms_norm(out, subln_g) * (1.0 - lambda_init)


def dilated_branch(q, k, v, window, dil, bias_table):
    B, H, S, d = q.shape
    R = window // (2 * dil)
    L = S // dil
    nb = -(-L // R)
    Lp = nb * R

    def to_sub(t):
        return t.reshape(B, H, L, dil, d).transpose(0, 1, 3, 2, 4)

    pad_q = ((0, 0), (0, 0), (0, 0), (0, Lp - L), (0, 0))
    pad_kv = ((0, 0), (0, 0), (0, 0), (R, Lp - L + R), (0, 0))
    qb = jnp.pad(to_sub(q), pad_q).reshape(B, H, dil, nb, R, d)

    def bands(t):
        tb = jnp.pad(to_sub(t), pad_kv).reshape(B, H, dil, nb + 2, R, d)
        return jnp.concatenate([tb[:, :, :, :-2], tb[:, :, :, 1:-1], tb[:, :, :, 2:]], axis=4)

    kb, vb = bands(k), bands(v)
    qi = jnp.arange(nb)[:, None] * R + jnp.arange(R)[None, :]
    kj = jnp.arange(nb)[:, None] * R - R + jnp.arange(3 * R)[None, :]
    rel = kj[:, None, :] - qi[:, :, None]
    valid = (jnp.abs(rel) <= R) & (kj[:, None, :] >= 0) & (kj[:, None, :] < L)
    bias = jnp.moveaxis(bias_table.astype(jnp.float32)[t5_bucket(rel * dil)], -1, 0)
    logits = jnp.einsum('bhrnqd,bhrnkd->bhrnqk', qb, kb).astype(jnp.float32) * (d ** -0.5)
    logits = jnp.where(valid, logits + bias[None, :, None], NEG_BIG)
    m = jnp.max(logits, axis=-1, keepdims=True)
    e = jnp.exp(logits - m)
    s = jnp.sum(e, axis=-1, keepdims=True)
    out = jnp.einsum('bhrnqk,bhrnkd->bhrnqd', (e / s).astype(v.dtype), vb)
    lse = (m + jnp.log(s))[..., 0]
    out = out.reshape(B, H, dil, Lp, d)[:, :, :, :L].transpose(0, 1, 3, 2, 4).reshape(B, H, S, d)
    lse = lse.reshape(B, H, dil, Lp)[:, :, :, :L].transpose(0, 1, 3, 2).reshape(B, H, S)
    return out, lse


def dilated_attention(q, k, v, bias_table):
    outs, lses = [], []
    for window, dil in DIL_PATTERNS:
        o, l = dilated_branch(q, k, v, window, dil, bias_table)
        outs.append(o)
        lses.append(l)
    w = jax.nn.softmax(jnp.stack(lses, axis=0), axis=0)
    return jnp.sum(w[..., None] * jnp.stack(outs, axis=0).astype(jnp.float32), axis=0).astype(q.dtype)


def dense_attention(q, k, v):
    B, H, S, dq = q.shape
    nblk = S // Q_BLOCK
    qb = q.reshape(B, H, nblk, Q_BLOCK, dq).transpose(2, 0, 1, 3, 4)
    scale = dq ** -0.5

    def block(qi):
        logits = jnp.einsum('bhqd,bhkd->bhqk', qi, k).astype(jnp.float32) * scale
        p = jax.nn.softmax(logits, axis=-1)
        return jnp.einsum('bhqk,bhkd->bhqd', p.astype(v.dtype), v)

    out = lax.map(block, qb)
    return out.transpose(1, 2, 0, 3, 4).reshape(B, H, S, v.shape[-1])


def mla_mixer(q_down, kv_down, q_norm_g, q_up, kv_norm_g, kv_up, qn_g, kn_g):
    B, S, _ = q_down.shape
    qk_dim = MLA_NOPE_DIM + MLA_ROPE_DIM
    q = (rms_norm(q_down, q_norm_g) @ q_up).reshape(B, S, MLA_HEADS, qk_dim)
    c_kv = rms_norm(kv_down[..., :MLA_KV_RANK], kv_norm_g)
    k_rope = kv_down[..., MLA_KV_RANK:]
    kv = (c_kv @ kv_up).reshape(B, S, MLA_HEADS, MLA_NOPE_DIM + MLA_V_DIM)
    k = jnp.concatenate([kv[..., :MLA_NOPE_DIM],
                         jnp.broadcast_to(k_rope[:, :, None, :], (B, S, MLA_HEADS, MLA_ROPE_DIM))], axis=-1)
    v = kv[..., MLA_NOPE_DIM:].transpose(0, 2, 1, 3)
    q = rms_norm(q, qn_g).transpose(0, 2, 1, 3)
    k = rms_norm(k, kn_g).transpose(0, 2, 1, 3)
    pos = jnp.arange(S)
    q = jnp.concatenate([q[..., :MLA_NOPE_DIM], rope(q[..., MLA_NOPE_DIM:], pos)], axis=-1)
    k = jnp.concatenate([k[..., :MLA_NOPE_DIM], rope(k[..., MLA_NOPE_DIM:], pos)], axis=-1)
    out = dense_attention(q, k, v)
    return out.transpose(0, 2, 1, 3).reshape(B, S, MLA_WIDTH)


def setup_inputs(seed: int = 0) -> dict:
    key = jax.random.key(seed)
    keys = iter(jax.random.split(key, 32))
    L = DEPTH

    def w(shape, fan_in):
        return jax.random.normal(next(keys), shape, jnp.float32) * fan_in ** -0.5

    def g(shape):
        return 1.0 + 0.05 * jax.random.normal(next(keys), shape, jnp.float32)

    return {
        'x': jax.random.normal(next(keys), (BATCH, SEQ, D_MODEL), jnp.float32),
        'rel_bias': 0.5 * jax.random.normal(next(keys), (NUM_BUCKETS, BIAS_HEADS), jnp.float32),
        'ffn1_norm': g((L, D_MODEL)),
        'ffn1_wg': w((L, D_MODEL, D_FF), D_MODEL),
        'ffn1_wu': w((L, D_MODEL, D_FF), D_MODEL),
        'ffn1_wd': w((L, D_FF, D_MODEL), D_FF),
        'mix_norm': g((L, D_MODEL)),
        'w_in': w((L, D_MODEL, IN_WIDTH), D_MODEL),
        'diff_q_norm': g((L, DIFF_QK_DIM)),
        'diff_k_norm': g((L, DIFF_QK_DIM)),
        'diff_lambda': 0.1 * jax.random.normal(next(keys), (L, 4, DIFF_QK_DIM), jnp.float32),
        'diff_subln': g((L, DIFF_V_DIM)),
        'dil_q_norm': g((L, DIL_HEAD_DIM)),
        'dil_k_norm': g((L, DIL_HEAD_DIM)),
        'mla_q_norm': g((L, MLA_Q_RANK)),
        'mla_q_up': w((L, MLA_Q_RANK, MLA_HEADS * (MLA_NOPE_DIM + MLA_ROPE_DIM)), MLA_Q_RANK),
        'mla_kv_norm': g((L, MLA_KV_RANK)),
        'mla_kv_up': w((L, MLA_KV_RANK, MLA_HEADS * (MLA_NOPE_DIM + MLA_V_DIM)), MLA_KV_RANK),
        'mla_qn': g((L, MLA_NOPE_DIM + MLA_ROPE_DIM)),
        'mla_kn': g((L, MLA_NOPE_DIM + MLA_ROPE_DIM)),
        'w_o': w((L, MIX_WIDTH, D_MODEL), MIX_WIDTH),
        'ffn2_norm': g((L, D_MODEL)),
        'ffn2_wg': w((L, D_MODEL, D_FF), D_MODEL),
        'ffn2_wu': w((L, D_MODEL, D_FF), D_MODEL),
        'ffn2_wd': w((L, D_FF, D_MODEL), D_FF),
    }


def reference(x, rel_bias, ffn1_norm, ffn1_wg, ffn1_wu, ffn1_wd, mix_norm, w_in,
              diff_q_norm, diff_k_norm, diff_lambda, diff_subln, dil_q_norm, dil_k_norm,
              mla_q_norm, mla_q_up, mla_kv_norm, mla_kv_up, mla_qn, mla_kn, w_o,
              ffn2_norm, ffn2_wg, ffn2_wu, ffn2_wd):
    B, S, _ = x.shape
    split_points = [int(p) for p in np.cumsum(IN_SIZES)[:-1]]
    diff_bias = rel_bias[:, :DIFF_HEADS]
    dil_bias = rel_bias[:, DIFF_HEADS:]
    for l in range(DEPTH):
        lambda_init = 0.8 - 0.6 * math.exp(-0.3 * (l + 1))
        x = x + 0.5 * swiglu(rms_norm(x, ffn1_norm[l]), ffn1_wg[l], ffn1_wu[l], ffn1_wd[l])
        h = rms_norm(x, mix_norm[l])
        dq, dk, dv, lq, lk, lv, mq, mkv = jnp.split(h @ w_in[l], split_points, axis=-1)
        dq = rms_norm(dq.reshape(B, S, DIFF_HEADS, 2, DIFF_QK_DIM), diff_q_norm[l]).transpose(0, 2, 1, 3, 4)
        dk = rms_norm(dk.reshape(B, S, DIFF_HEADS, 2, DIFF_QK_DIM), diff_k_norm[l]).transpose(0, 2, 1, 3, 4)
        dv = dv.reshape(B, S, DIFF_HEADS, DIFF_V_DIM).transpose(0, 2, 1, 3)
        lp = diff_lambda[l].astype(jnp.float32)
        lam = jnp.exp(jnp.sum(lp[0] * lp[1])) - jnp.exp(jnp.sum(lp[2] * lp[3])) + lambda_init
        out_a = diff_attention(dq, dk, dv, lam, lambda_init, diff_subln[l], diff_bias)
        out_a = out_a.transpose(0, 2, 1, 3).reshape(B, S, DIFF_WIDTH)
        lq = rms_norm(lq.reshape(B, S, DIL_HEADS, DIL_HEAD_DIM), dil_q_norm[l]).transpose(0, 2, 1, 3)
        lk = rms_norm(lk.reshape(B, S, DIL_HEADS, DIL_HEAD_DIM), dil_k_norm[l]).transpose(0, 2, 1, 3)
        lv = lv.reshape(B, S, DIL_HEADS, DIL_HEAD_DIM).transpose(0, 2, 1, 3)
        out_b = dilated_attention(lq, lk, lv, dil_bias).transpose(0, 2, 1, 3).reshape(B, S, DIL_WIDTH)
        out_c = mla_mixer(mq, mkv, mla_q_norm[l], mla_q_up[l], mla_kv_norm[l], mla_kv_up[l], mla_qn[l], mla_kn[l])
        mix = jnp.concatenate([out_a, out_b, out_c], axis=-1)
        x = x + mix @ w_o[l]
        x = x + 0.5 * swiglu(rms_norm(x, ffn2_norm[l]), ffn2_wg[l], ffn2_wu[l], ffn2_wd[l])
    return x
```

```python
import functools
import math

import jax
import jax.numpy as jnp
import numpy as np
from jax import lax
from jax.experimental import pallas as pl
from jax.experimental.pallas import tpu as pltpu

F32 = jnp.float32
BF16 = jnp.bfloat16

D_MODEL = 1024
SEQ = 2048
DEPTH = 2
EPS = 1e-6
D_FF = 2816
NUM_BUCKETS = 32
MAX_DISTANCE = 1024
ROPE_THETA = 10000.0
NEG_BIG = -1e30
DIFF_HEADS, DIFF_QK_DIM, DIFF_V_DIM = 4, 32, 64
DIL_HEADS, DIL_HEAD_DIM = 6, 64
DIL_PATTERNS = ((128, 1), (512, 4), (2048, 16))
MLA_HEADS, MLA_Q_RANK, MLA_KV_RANK = 6, 256, 128
MLA_NOPE_DIM, MLA_ROPE_DIM, MLA_V_DIM = 64, 32, 64
MLA_QK_DIM = MLA_NOPE_DIM + MLA_ROPE_DIM
DIFF_WIDTH = DIFF_HEADS * DIFF_V_DIM
DIL_WIDTH = DIL_HEADS * DIL_HEAD_DIM
MLA_WIDTH = MLA_HEADS * MLA_V_DIM
BIAS_HEADS = DIFF_HEADS + DIL_HEADS

LANE = 128
SUBLANE = 8
VMEM_LIMIT_BYTES = 56 * 1024 * 1024

FFN_TM = 1024
FFN_TF = 256
PROJ_TM = 512
OUT_TM = 1024
ATT_TQ = 256
STRIP_W = 2 * SEQ
STRIP_KEEP = 2 * SEQ - ATT_TQ
N_QBLK = SEQ // ATT_TQ

SEG_DQ, SEG_DK, SEG_DV = 0, 256, 512
SEG_LQ, SEG_LK, SEG_LV = 768, 1152, 1536
SEG_MQ, SEG_MKV, SEG_ROPE = 1920, 2176, 2304
IN_EXT = 2432
MLA_PAD = MLA_HEADS * LANE


def _params(*sem):
    return pltpu.CompilerParams(dimension_semantics=sem, vmem_limit_bytes=VMEM_LIMIT_BYTES)


def _rms(x, g):
    return x * lax.rsqrt(jnp.mean(x * x, axis=-1, keepdims=True) + EPS) * g


def _ffn_kernel(x_ref, g_ref, wg_ref, wu_ref, wd_ref, o_ref, n_scr, acc_scr):
    j = pl.program_id(1)

    @pl.when(j == 0)
    def _():
        n_scr[...] = _rms(x_ref[...], g_ref[...]).astype(BF16)
        acc_scr[...] = jnp.zeros_like(acc_scr)

    n = n_scr[...]
    gate = jnp.dot(n, wg_ref[...], preferred_element_type=F32)
    up = jnp.dot(n, wu_ref[...], preferred_element_type=F32)
    h = (gate * jax.nn.sigmoid(gate) * up).astype(BF16)
    acc_scr[...] += jnp.dot(h, wd_ref[...], preferred_element_type=F32)

    @pl.when(j == pl.num_programs(1) - 1)
    def _():
        o_ref[...] = x_ref[...] + 0.5 * acc_scr[...]


def _ffn(x2d, g, wg, wu, wd):
    t = x2d.shape[0]
    return pl.pallas_call(
        _ffn_kernel,
        name="ffn",
        grid=(t // FFN_TM, D_FF // FFN_TF),
        in_specs=[
            pl.BlockSpec((FFN_TM, D_MODEL), lambda i, j: (i, 0)),
            pl.BlockSpec((1, D_MODEL), lambda i, j: (0, 0)),
            pl.BlockSpec((D_MODEL, FFN_TF), lambda i, j: (0, j)),
            pl.BlockSpec((D_MODEL, FFN_TF), lambda i, j: (0, j)),
            pl.BlockSpec((FFN_TF, D_MODEL), lambda i, j: (j, 0)),
        ],
        out_specs=pl.BlockSpec((FFN_TM, D_MODEL), lambda i, j: (i, 0)),
        out_shape=jax.ShapeDtypeStruct((t, D_MODEL), F32),
        scratch_shapes=[pltpu.VMEM((FFN_TM, D_MODEL), BF16), pltpu.VMEM((FFN_TM, D_MODEL), F32)],
        compiler_params=_params("parallel", "arbitrary"),
    )(x2d, g, wg, wu, wd)


def _group_norm(x, gsum, inv_n, gain):
    ss = jnp.dot((x * x).astype(BF16), gsum, preferred_element_type=F32)
    return x * lax.rsqrt(ss * inv_n + EPS) * gain


def _group_norm_chunks(x, gsum, inv_n, gain):
    w = gsum.shape[0]
    parts = [_group_norm(x[:, c:c + w], gsum, inv_n, gain[:, c:c + w]) for c in range(0, x.shape[1], w)]
    return jnp.concatenate(parts, axis=1)


def _rope_blocks(x, cos, sin):
    lane = lax.broadcasted_iota(jnp.int32, (x.shape[0], LANE), 1)
    first_half = lane < MLA_NOPE_DIM + MLA_ROPE_DIM // 2
    parts = []
    for c in range(0, x.shape[1], LANE):
        xb = x[:, c:c + LANE]
        partner = jnp.where(first_half,
                            pltpu.roll(xb, LANE - MLA_ROPE_DIM // 2, 1),
                            pltpu.roll(xb, MLA_ROPE_DIM // 2, 1))
        parts.append(xb * cos + partner * sin)
    return jnp.concatenate(parts, axis=1)


def _proj_kernel(x_ref, g_ref, win_ref, qup_ref, kvk_ref, kvv_ref,
                 gdq_ref, gdk_ref, glq_ref, glk_ref, gmq_ref, gmkv_ref, gqn_ref, gkn_ref,
                 cos_ref, sin_ref, g32_ref, g64_ref, g128_ref,
                 dq_ref, dk_ref, dv_ref, lq_ref, lk_ref, lv_ref, mq_ref, mk_ref, mv_ref):
    h = _rms(x_ref[...], g_ref[...]).astype(BF16)

    def seg(lo, hi):
        return jnp.dot(h, win_ref[:, lo:hi], preferred_element_type=F32)

    g32, g64, g128 = g32_ref[...], g64_ref[...], g128_ref[...]
    dq_ref[...] = _group_norm(seg(SEG_DQ, SEG_DK), g32, 1.0 / DIFF_QK_DIM, gdq_ref[...]).astype(BF16)
    dk_ref[...] = _group_norm(seg(SEG_DK, SEG_DV), g32, 1.0 / DIFF_QK_DIM, gdk_ref[...]).astype(BF16)
    dv_ref[...] = seg(SEG_DV, SEG_LQ).astype(BF16)
    lq_ref[...] = _group_norm(seg(SEG_LQ, SEG_LK), g64, 1.0 / DIL_HEAD_DIM, glq_ref[...]).astype(BF16)
    lk_ref[...] = _group_norm(seg(SEG_LK, SEG_LV), g64, 1.0 / DIL_HEAD_DIM, glk_ref[...]).astype(BF16)
    lv_ref[...] = seg(SEG_LV, SEG_MQ).astype(BF16)

    cos, sin = cos_ref[...], sin_ref[...]
    q_lat = _rms(seg(SEG_MQ, SEG_MKV), gmq_ref[...]).astype(BF16)
    q = jnp.dot(q_lat, qup_ref[...], preferred_element_type=F32)
    q = _group_norm_chunks(q, g128, 1.0 / MLA_QK_DIM, gqn_ref[...])
    mq_ref[...] = _rope_blocks(q, cos, sin).astype(BF16)

    c_kv = _rms(seg(SEG_MKV, SEG_ROPE), gmkv_ref[...]).astype(BF16)
    k_rope = seg(SEG_ROPE, IN_EXT)
    k = jnp.dot(c_kv, kvk_ref[...], preferred_element_type=F32)
    k = k + jnp.concatenate([k_rope] * MLA_HEADS, axis=1)
    k = _group_norm_chunks(k, g128, 1.0 / MLA_QK_DIM, gkn_ref[...])
    mk_ref[...] = _rope_blocks(k, cos, sin).astype(BF16)
    mv_ref[...] = jnp.dot(c_kv, kvv_ref[...], preferred_element_type=F32).astype(BF16)


def _block_diag_ones(width, group):
    idx = np.arange(width) // group
    return jnp.asarray((idx[:, None] == idx[None, :]).astype(np.float32), dtype=BF16)


def _proj(x2d, pw):
    t = x2d.shape[0]
    tm = PROJ_TM
    row = lambda i: (i, 0)
    fixed = lambda i: (0, 0)
    pos_blocks = SEQ // tm

    def full(a):
        return pl.BlockSpec(a.shape, fixed)

    consts = [pw["g"], pw["w_in"], pw["q_up"], pw["kv_k"], pw["kv_v"],
              pw["gdq"], pw["gdk"], pw["glq"], pw["glk"], pw["gmq"], pw["gmkv"], pw["gqn"], pw["gkn"]]
    tables = [pw["cos"], pw["sin"]]
    gsums = [_block_diag_ones(256, DIFF_QK_DIM), _block_diag_ones(DIL_WIDTH, DIL_HEAD_DIM),
             _block_diag_ones(256, LANE)]
    widths = [256, 256, 256, DIL_WIDTH, DIL_WIDTH, DIL_WIDTH, MLA_PAD, MLA_PAD, MLA_WIDTH]
    return pl.pallas_call(
        _proj_kernel,
        name="proj",
        grid=(t // tm,),
        in_specs=([pl.BlockSpec((tm, D_MODEL), row)] + [full(a) for a in consts]
                  + [pl.BlockSpec((tm, LANE), lambda i: (i % pos_blocks, 0)) for _ in tables]
                  + [full(a) for a in gsums]),
        out_specs=[pl.BlockSpec((tm, w), row) for w in widths],
        out_shape=[jax.ShapeDtypeStruct((t, w), BF16) for w in widths],
        compiler_params=_params("parallel"),
    )(x2d, *consts, *tables, *gsums)


def _strip_kernel(x0_ref, o_ref):
    x0 = x0_ref[0]
    for a in range(ATT_TQ // SUBLANE):
        rolled = x0 if a == 0 else pltpu.roll(x0, SUBLANE * a, 1)
        o_ref[0, a * SUBLANE:(a + 1) * SUBLANE, :] = rolled[:, :STRIP_KEEP]


def _bias_strips(x0):
    nh = x0.shape[0]
    return pl.pallas_call(
        _strip_kernel,
        name="bias_strips",
        grid=(nh,),
        in_specs=[pl.BlockSpec((1, SUBLANE, STRIP_W), lambda h: (h, 0, 0))],
        out_specs=pl.BlockSpec((1, ATT_TQ, STRIP_KEEP), lambda h: (h, 0, 0)),
        out_shape=jax.ShapeDtypeStruct((nh, ATT_TQ, STRIP_KEEP), F32),
        compiler_params=_params("parallel"),
    )(x0)


def _t5_bucket(rel):
    half = NUM_BUCKETS // 2
    max_exact = half // 2
    n = jnp.abs(rel)
    nf = jnp.maximum(n, 1).astype(F32)
    large = max_exact + (jnp.log(nf / max_exact) / math.log(MAX_DISTANCE / max_exact)
                         * (half - max_exact)).astype(jnp.int32)
    large = jnp.minimum(large, half - 1)
    return jnp.where(rel > 0, half, 0) + jnp.where(n < max_exact, n, large)


def _bias_rows(rel_bias):
    rel = np.arange(-(SEQ - 1), SEQ)
    mult = np.zeros(rel.shape, np.int64)
    for window, dil in DIL_PATTERNS:
        mult += (rel % dil == 0) & (np.abs(rel) // dil <= window // (2 * dil))
    table = rel_bias.astype(F32)[_t5_bucket(jnp.asarray(rel, jnp.int32))]
    logm = jnp.asarray(np.log(np.maximum(mult, 1)), F32)[:, None]
    dil_vec = jnp.where(jnp.asarray(mult > 0)[:, None], table[:, DIFF_HEADS:] + logm, NEG_BIG)
    vec = jnp.concatenate([table[:, :DIFF_HEADS], dil_vec], axis=1).T
    s = np.arange(SUBLANE)[:, None]
    x = (np.arange(STRIP_W)[None, :] - s) % STRIP_W
    d = np.where(x < STRIP_KEEP, x, x - STRIP_W)
    idx = np.clip(d + ATT_TQ - 1, 0, 2 * SEQ - 2)
    return vec[:, idx]


_NT = (((1,), (1,)), ((), ()))


def _lanes_between(shape, lo, hi):
    lane = lax.broadcasted_iota(jnp.int32, shape, 1)
    return (lane >= lo) & (lane < hi)


def _softmax_pv(s, vm):
    m = jnp.max(s, axis=-1, keepdims=True)
    e = jnp.exp(s - m)
    l = jnp.sum(e, axis=-1, keepdims=True)
    pv = jnp.dot(e.astype(BF16), vm, preferred_element_type=F32)
    return pv * (1.0 / l)


def _load_strip(strip_ref, bias_scr):
    qb = pl.program_id(1)
    for blk in range(N_QBLK):
        off = (N_QBLK - 1 - blk) * ATT_TQ

        @pl.when((qb == blk) & (pl.program_id(2) == 0))
        def _():
            bias_scr[...] = strip_ref[:, :, off:off + SEQ]


def _diff_kernel(q_ref, k_ref, v_ref, strip_ref, lam_ref, g_ref, o_ref, bias_scr, *, lambda_init):
    _load_strip(strip_ref, bias_scr)
    q, k, v = q_ref[0], k_ref[0], v_ref[0]
    lp = lam_ref[...]
    lam = (jnp.exp(jnp.sum(lp[0:1] * lp[1:2], axis=-1, keepdims=True))
           - jnp.exp(jnp.sum(lp[2:3] * lp[3:4], axis=-1, keepdims=True)) + lambda_init)
    out = jnp.zeros((ATT_TQ, LANE), F32)
    for hh in range(2):
        vm = jnp.where(_lanes_between(v.shape, DIFF_V_DIM * hh, DIFF_V_DIM * (hh + 1)), v, jnp.zeros_like(v))
        halves = []
        for c in range(2):
            lo = 2 * DIFF_QK_DIM * hh + DIFF_QK_DIM * c
            qm = jnp.where(_lanes_between(q.shape, lo, lo + DIFF_QK_DIM), q, jnp.zeros_like(q))
            s = lax.dot_general(qm, k, _NT, preferred_element_type=F32) + bias_scr[hh]
            halves.append(_softmax_pv(s, vm))
        out = out + (halves[0] - lam * halves[1])
    sq = out * out
    first = _lanes_between(out.shape, 0, DIFF_V_DIM)
    ss0 = jnp.sum(jnp.where(first, sq, 0.0), axis=-1, keepdims=True)
    ss1 = jnp.sum(jnp.where(first, 0.0, sq), axis=-1, keepdims=True)
    ss = jnp.where(first, ss0, ss1)
    y = out * lax.rsqrt(ss * (1.0 / DIFF_V_DIM) + EPS) * g_ref[...] * (1.0 - lambda_init)
    o_ref[0] = y.astype(BF16)


def _dil_kernel(q_ref, k_ref, v_ref, strip_ref, o_ref, bias_scr):
    _load_strip(strip_ref, bias_scr)
    q, k, v = q_ref[0], k_ref[0], v_ref[0]
    out = jnp.zeros((ATT_TQ, LANE), F32)
    for hh in range(2):
        sel = (DIL_HEAD_DIM * hh, DIL_HEAD_DIM * (hh + 1))
        vm = jnp.where(_lanes_between(v.shape, *sel), v, jnp.zeros_like(v))
        qm = jnp.where(_lanes_between(q.shape, *sel), q, jnp.zeros_like(q))
        s = lax.dot_general(qm, k, _NT, preferred_element_type=F32) + bias_scr[hh]
        out = out + _softmax_pv(s, vm)
    o_ref[0] = out.astype(BF16)


def _mla_kernel(q_ref, k_ref, v_ref, o_ref):
    q, k, v = q_ref[0], k_ref[0], v_ref[0]
    out = jnp.zeros((ATT_TQ, LANE), F32)
    for hh in range(2):
        vm = jnp.where(_lanes_between(v.shape, MLA_V_DIM * hh, MLA_V_DIM * (hh + 1)), v, jnp.zeros_like(v))
        s = lax.dot_general(q[:, hh * LANE:(hh + 1) * LANE], k[:, hh * LANE:(hh + 1) * LANE], _NT,
                            preferred_element_type=F32)
        out = out + _softmax_pv(s, vm)
    o_ref[0] = out.astype(BF16)


def _biased_attention(kernel, name, q, k, v, strips, head0, extra=()):
    b, s, w = q.shape
    n_pairs = w // LANE
    blk = lambda p, i, bb: (bb, i, p)
    kv = lambda p, i, bb: (bb, 0, p)
    extra_specs = [pl.BlockSpec(a.shape if a.shape[-1] != w else (1, LANE),
                                (lambda p, i, bb: (0, p)) if a.shape[-1] == w else (lambda p, i, bb: (0, 0)))
                   for a in extra]
    return pl.pallas_call(
        kernel,
        name=name,
        grid=(n_pairs, N_QBLK, b),
        in_specs=[
            pl.BlockSpec((1, ATT_TQ, LANE), blk),
            pl.BlockSpec((1, s, LANE), kv),
            pl.BlockSpec((1, s, LANE), kv),
            pl.BlockSpec((2, ATT_TQ, STRIP_KEEP), lambda p, i, bb: (head0 // 2 + p, 0, 0)),
        ] + extra_specs,
        out_specs=pl.BlockSpec((1, ATT_TQ, LANE), blk),
        out_shape=jax.ShapeDtypeStruct((b, s, w), BF16),
        scratch_shapes=[pltpu.VMEM((2, ATT_TQ, s), F32)],
        compiler_params=_params("arbitrary", "arbitrary", "arbitrary"),
    )(q, k, v, strips, *extra)


def _mla_attention(q, k, v):
    b, s, w = v.shape
    n_pairs = w // LANE
    return pl.pallas_call(
        _mla_kernel,
        name="mla_attn",
        grid=(b, n_pairs, N_QBLK),
        in_specs=[
            pl.BlockSpec((1, ATT_TQ, 2 * LANE), lambda bb, p, i: (bb, i, p)),
            pl.BlockSpec((1, s, 2 * LANE), lambda bb, p, i: (bb, 0, p)),
            pl.BlockSpec((1, s, LANE), lambda bb, p, i: (bb, 0, p)),
        ],
        out_specs=pl.BlockSpec((1, ATT_TQ, LANE), lambda bb, p, i: (bb, i, p)),
        out_shape=jax.ShapeDtypeStruct((b, s, w), BF16),
        compiler_params=_params("parallel", "parallel", "parallel"),
    )(q, k, v)


def _out_kernel(x_ref, a_ref, b_ref, c_ref, wa_ref, wb_ref, wc_ref, o_ref):
    mix = jnp.dot(a_ref[...], wa_ref[...], preferred_element_type=F32)
    mix += jnp.dot(b_ref[...], wb_ref[...], preferred_element_type=F32)
    mix += jnp.dot(c_ref[...], wc_ref[...], preferred_element_type=F32)
    o_ref[...] = x_ref[...] + mix


def _out_proj(x2d, a, b, c, wo):
    t = x2d.shape[0]
    tm = OUT_TM
    row = lambda i: (i, 0)
    fixed = lambda i: (0, 0)
    wa, wb, wc = wo[:DIFF_WIDTH], wo[DIFF_WIDTH:DIFF_WIDTH + DIL_WIDTH], wo[DIFF_WIDTH + DIL_WIDTH:]
    return pl.pallas_call(
        _out_kernel,
        name="out_proj",
        grid=(t // tm,),
        in_specs=[pl.BlockSpec((tm, D_MODEL), row),
                  pl.BlockSpec((tm, DIFF_WIDTH), row), pl.BlockSpec((tm, DIL_WIDTH), row),
                  pl.BlockSpec((tm, MLA_WIDTH), row),
                  pl.BlockSpec(wa.shape, fixed), pl.BlockSpec(wb.shape, fixed), pl.BlockSpec(wc.shape, fixed)],
        out_specs=pl.BlockSpec((tm, D_MODEL), row),
        out_shape=jax.ShapeDtypeStruct((t, D_MODEL), F32),
        compiler_params=_params("parallel"),
    )(x2d, a, b, c, wa, wb, wc)


def _rope_tables():
    half = MLA_ROPE_DIM // 2
    inv = ROPE_THETA ** (-jnp.arange(half, dtype=F32) / half)
    ang = jnp.arange(SEQ).astype(F32)[:, None] * inv[None, :]
    cos, sin = jnp.cos(ang), jnp.sin(ang)
    ones = jnp.ones((SEQ, MLA_NOPE_DIM), F32)
    tail = LANE - MLA_QK_DIM
    cos_t = jnp.concatenate([ones, cos, cos, jnp.ones((SEQ, tail), F32)], axis=1)
    sin_t = jnp.concatenate([0.0 * ones, -sin, sin, jnp.zeros((SEQ, tail), F32)], axis=1)
    return cos_t, sin_t


def _proj_weights(l, mix_norm, w_in, diff_q_norm, diff_k_norm, dil_q_norm, dil_k_norm,
                  mla_q_norm, mla_q_up, mla_kv_norm, mla_kv_up, mla_qn, mla_kn, cos_t, sin_t):
    wi = w_in[l]
    pad_to_block = LANE - MLA_QK_DIM
    w_ext = jnp.concatenate([wi[:, :SEG_ROPE], jnp.zeros((D_MODEL, MLA_NOPE_DIM), F32),
                             wi[:, SEG_ROPE:], jnp.zeros((D_MODEL, pad_to_block), F32)], axis=1)
    q_up = jnp.pad(mla_q_up[l].reshape(MLA_Q_RANK, MLA_HEADS, MLA_QK_DIM), ((0, 0), (0, 0), (0, pad_to_block)))
    kv = mla_kv_up[l].reshape(MLA_KV_RANK, MLA_HEADS, MLA_NOPE_DIM + MLA_V_DIM)
    kv_k = jnp.pad(kv[:, :, :MLA_NOPE_DIM], ((0, 0), (0, 0), (0, LANE - MLA_NOPE_DIM)))
    kv_v = kv[:, :, MLA_NOPE_DIM:]

    def head_gain(g, heads, scale=1.0, pad=0):
        return jnp.tile(jnp.pad(g.astype(F32) * scale, (0, pad)), heads)[None, :]

    return {
        "g": mix_norm[l][None, :],
        "w_in": w_ext.astype(BF16),
        "q_up": q_up.reshape(MLA_Q_RANK, MLA_PAD).astype(BF16),
        "kv_k": kv_k.reshape(MLA_KV_RANK, MLA_PAD).astype(BF16),
        "kv_v": kv_v.reshape(MLA_KV_RANK, MLA_WIDTH).astype(BF16),
        "gdq": head_gain(diff_q_norm[l], 2 * DIFF_HEADS, DIFF_QK_DIM ** -0.5),
        "gdk": head_gain(diff_k_norm[l], 2 * DIFF_HEADS),
        "glq": head_gain(dil_q_norm[l], DIL_HEADS, DIL_HEAD_DIM ** -0.5),
        "glk": head_gain(dil_k_norm[l], DIL_HEADS),
        "gmq": mla_q_norm[l][None, :],
        "gmkv": mla_kv_norm[l][None, :],
        "gqn": head_gain(mla_qn[l], MLA_HEADS, MLA_QK_DIM ** -0.5, pad_to_block),
        "gkn": head_gain(mla_kn[l], MLA_HEADS, 1.0, pad_to_block),
        "cos": cos_t,
        "sin": sin_t,
    }


def kernel(x, rel_bias, ffn1_norm, ffn1_wg, ffn1_wu, ffn1_wd, mix_norm, w_in, diff_q_norm, diff_k_norm, diff_lambda, diff_subln, dil_q_norm, dil_k_norm, mla_q_norm, mla_q_up, mla_kv_norm, mla_kv_up, mla_qn, mla_kn, w_o, ffn2_norm, ffn2_wg, ffn2_wu, ffn2_wd):
    b, s, d = x.shape
    assert (s, d) == (SEQ, D_MODEL)
    t = b * s
    strips = _bias_strips(_bias_rows(rel_bias))
    cos_t, sin_t = _rope_tables()
    x2d = x.reshape(t, d)
    for l in range(DEPTH):
        lambda_init = 0.8 - 0.6 * math.exp(-0.3 * (l + 1))
        x2d = _ffn(x2d, ffn1_norm[l][None, :], ffn1_wg[l].astype(BF16), ffn1_wu[l].astype(BF16),
                   ffn1_wd[l].astype(BF16))
        pw = _proj_weights(l, mix_norm, w_in, diff_q_norm, diff_k_norm, dil_q_norm, dil_k_norm,
                           mla_q_norm, mla_q_up, mla_kv_norm, mla_kv_up, mla_qn, mla_kn, cos_t, sin_t)
        dq, dk, dv, lq, lk, lv, mq, mk, mv = [a.reshape(b, s, -1) for a in _proj(x2d, pw)]
        subln = jnp.tile(diff_subln[l].astype(F32), DIFF_HEADS)[None, :]
        out_a = _biased_attention(functools.partial(_diff_kernel, lambda_init=lambda_init), "diff_attn",
                                  dq, dk, dv, strips, 0, extra=(diff_lambda[l].astype(F32), subln))
        out_b = _biased_attention(_dil_kernel, "dil_attn", lq, lk, lv, strips, DIFF_HEADS)
        out_c = _mla_attention(mq, mk, mv)
        x2d = _out_proj(x2d, out_a.reshape(t, -1), out_b.reshape(t, -1), out_c.reshape(t, -1),
                        w_o[l].astype(BF16))
        x2d = _ffn(x2d, ffn2_norm[l][None, :], ffn2_wg[l].astype(BF16), ffn2_wu[l].astype(BF16),
                   ffn2_wd[l].astype(BF16))
    return x2d.reshape(b, s, d)
```

```python
import functools
import math

import jax
import jax.numpy as jnp
import numpy as np
from jax import lax
from jax.experimental import pallas as pl
from jax.experimental.pallas import tpu as pltpu

F32 = jnp.float32
BF16 = jnp.bfloat16

D_MODEL = 1024
SEQ = 2048
DEPTH = 2
EPS = 1e-6
D_FF = 2816
NUM_BUCKETS = 32
MAX_DISTANCE = 1024
ROPE_THETA = 10000.0
NEG_BIG = -1e30
LOG2E = math.log2(math.e)
DIFF_HEADS, DIFF_QK_DIM, DIFF_V_DIM = 4, 32, 64
DIL_HEADS, DIL_HEAD_DIM = 6, 64
DIL_PATTERNS = ((128, 1), (512, 4), (2048, 16))
MLA_HEADS, MLA_Q_RANK, MLA_KV_RANK = 6, 256, 128
MLA_NOPE_DIM, MLA_ROPE_DIM, MLA_V_DIM = 64, 32, 64
MLA_QK_DIM = MLA_NOPE_DIM + MLA_ROPE_DIM
HEAD_V = 64
DIFF_WIDTH = DIFF_HEADS * DIFF_V_DIM
DIL_WIDTH = DIL_HEADS * DIL_HEAD_DIM
MLA_WIDTH = MLA_HEADS * MLA_V_DIM
BIAS_HEADS = DIFF_HEADS + DIL_HEADS

LANE = 128
SUBLANE = 8
BF16_ROWS = 16
VMEM_LIMIT_BYTES = 56 * 1024 * 1024

FFN_TM = 1024
FFN_TF = 256
PROJ_TM = 512
OUT_TM = 1024
ATT_TQ = 512
MAX_SLAB = 128
STRIP_ROWS = 2 * SEQ - ATT_TQ
STRIP_SRC_W = 2 * SEQ + LANE
N_QBLK = SEQ // ATT_TQ

SEG_DQ, SEG_DK, SEG_DV = 0, 256, 512
SEG_LQ, SEG_LK, SEG_LV = 768, 1152, 1536
SEG_MQ, SEG_MKV, SEG_ROPE = 1920, 2176, 2304
IN_EXT = 2432
MLA_PAD = MLA_HEADS * LANE


def _params(*sem):
    return pltpu.CompilerParams(dimension_semantics=sem, vmem_limit_bytes=VMEM_LIMIT_BYTES)


def _rms(x, g):
    return x * lax.rsqrt(jnp.mean(x * x, axis=-1, keepdims=True) + EPS) * g


def _ffn_kernel(x_ref, g_ref, wg_ref, wu_ref, wd_ref, o_ref, n_scr, acc_scr):
    j = pl.program_id(1)

    @pl.when(j == 0)
    def _():
        n_scr[...] = _rms(x_ref[...], g_ref[...]).astype(BF16)
        acc_scr[...] = jnp.zeros_like(acc_scr)

    n = n_scr[...]
    gate = jnp.dot(n, wg_ref[...], preferred_element_type=F32)
    up = jnp.dot(n, wu_ref[...], preferred_element_type=F32)
    h = (gate * jax.nn.sigmoid(gate) * up).astype(BF16)
    acc_scr[...] += jnp.dot(h, wd_ref[...], preferred_element_type=F32)

    @pl.when(j == pl.num_programs(1) - 1)
    def _():
        o_ref[...] = x_ref[...] + 0.5 * acc_scr[...]


def _ffn(x2d, g, wg, wu, wd):
    t = x2d.shape[0]
    return pl.pallas_call(
        _ffn_kernel,
        name="ffn",
        grid=(t // FFN_TM, D_FF // FFN_TF),
        in_specs=[
            pl.BlockSpec((FFN_TM, D_MODEL), lambda i, j: (i, 0)),
            pl.BlockSpec((1, D_MODEL), lambda i, j: (0, 0)),
            pl.BlockSpec((D_MODEL, FFN_TF), lambda i, j: (0, j)),
            pl.BlockSpec((D_MODEL, FFN_TF), lambda i, j: (0, j)),
            pl.BlockSpec((FFN_TF, D_MODEL), lambda i, j: (j, 0)),
        ],
        out_specs=pl.BlockSpec((FFN_TM, D_MODEL), lambda i, j: (i, 0)),
        out_shape=jax.ShapeDtypeStruct((t, D_MODEL), F32),
        scratch_shapes=[pltpu.VMEM((FFN_TM, D_MODEL), BF16), pltpu.VMEM((FFN_TM, D_MODEL), F32)],
        compiler_params=_params("parallel", "arbitrary"),
    )(x2d, g, wg, wu, wd)


def _group_norm(x, gsum, inv_n, gain):
    ss = jnp.dot((x * x).astype(BF16), gsum, preferred_element_type=F32)
    return x * lax.rsqrt(ss * inv_n + EPS) * gain


def _group_norm_chunks(x, gsum, inv_n, gain):
    w = gsum.shape[0]
    parts = [_group_norm(x[:, c:c + w], gsum, inv_n, gain[:, c:c + w]) for c in range(0, x.shape[1], w)]
    return jnp.concatenate(parts, axis=1)


def _rope_blocks(x, cos, sin):
    lane = lax.broadcasted_iota(jnp.int32, (x.shape[0], LANE), 1)
    first_half = lane < MLA_NOPE_DIM + MLA_ROPE_DIM // 2
    parts = []
    for c in range(0, x.shape[1], LANE):
        xb = x[:, c:c + LANE]
        partner = jnp.where(first_half,
                            pltpu.roll(xb, LANE - MLA_ROPE_DIM // 2, 1),
                            pltpu.roll(xb, MLA_ROPE_DIM // 2, 1))
        parts.append(xb * cos + partner * sin)
    return jnp.concatenate(parts, axis=1)


def _proj_kernel(x_ref, g_ref, win_ref, qup_ref, kvk_ref, kvv_ref,
                 gdq_ref, gdk_ref, glq_ref, glk_ref, gmq_ref, gmkv_ref, gqn_ref, gkn_ref,
                 cos_ref, sin_ref, g32_ref, g64_ref, g128_ref,
                 dq_ref, dk_ref, dvt_ref, lq_ref, lk_ref, lvt_ref, mq_ref, mk_ref, mvt_ref):
    h = _rms(x_ref[...], g_ref[...]).astype(BF16)

    def seg(lo, hi):
        return jnp.dot(h, win_ref[:, lo:hi], preferred_element_type=F32)

    g32, g64, g128 = g32_ref[...], g64_ref[...], g128_ref[...]
    dq_ref[...] = _group_norm(seg(SEG_DQ, SEG_DK), g32, 1.0 / DIFF_QK_DIM, gdq_ref[...]).astype(BF16)
    dk_ref[...] = _group_norm(seg(SEG_DK, SEG_DV), g32, 1.0 / DIFF_QK_DIM, gdk_ref[...]).astype(BF16)
    dvt_ref[0] = seg(SEG_DV, SEG_LQ).T.astype(BF16)
    lq_ref[...] = _group_norm(seg(SEG_LQ, SEG_LK), g64, 1.0 / DIL_HEAD_DIM, glq_ref[...]).astype(BF16)
    lk_ref[...] = _group_norm(seg(SEG_LK, SEG_LV), g64, 1.0 / DIL_HEAD_DIM, glk_ref[...]).astype(BF16)
    lvt_ref[0] = seg(SEG_LV, SEG_MQ).T.astype(BF16)

    cos, sin = cos_ref[...], sin_ref[...]
    q_lat = _rms(seg(SEG_MQ, SEG_MKV), gmq_ref[...]).astype(BF16)
    q = jnp.dot(q_lat, qup_ref[...], preferred_element_type=F32)
    q = _group_norm_chunks(q, g128, 1.0 / MLA_QK_DIM, gqn_ref[...])
    mq_ref[...] = _rope_blocks(q, cos, sin).astype(BF16)

    c_kv = _rms(seg(SEG_MKV, SEG_ROPE), gmkv_ref[...]).astype(BF16)
    k_rope = seg(SEG_ROPE, IN_EXT)
    k = jnp.dot(c_kv, kvk_ref[...], preferred_element_type=F32)
    k = k + jnp.concatenate([k_rope] * MLA_HEADS, axis=1)
    k = _group_norm_chunks(k, g128, 1.0 / MLA_QK_DIM, gkn_ref[...])
    mk_ref[...] = _rope_blocks(k, cos, sin).astype(BF16)
    mvt_ref[0] = jnp.dot(c_kv, kvv_ref[...], preferred_element_type=F32).T.astype(BF16)


def _block_diag_ones(width, group):
    idx = np.arange(width) // group
    return jnp.asarray((idx[:, None] == idx[None, :]).astype(np.float32), dtype=BF16)


def _proj(x2d, pw, batch):
    t = x2d.shape[0]
    tm = PROJ_TM
    row = lambda i: (i, 0)
    fixed = lambda i: (0, 0)
    pos_blocks = SEQ // tm
    col = lambda i: (i // pos_blocks, 0, i % pos_blocks)

    def full(a):
        return pl.BlockSpec(a.shape, fixed)

    consts = [pw["g"], pw["w_in"], pw["q_up"], pw["kv_k"], pw["kv_v"],
              pw["gdq"], pw["gdk"], pw["glq"], pw["glk"], pw["gmq"], pw["gmkv"], pw["gqn"], pw["gkn"]]
    tables = [pw["cos"], pw["sin"]]
    gsums = [_block_diag_ones(256, DIFF_QK_DIM), _block_diag_ones(DIL_WIDTH, DIL_HEAD_DIM),
             _block_diag_ones(256, LANE)]
    outs = [(256, False), (256, False), (DIFF_WIDTH, True), (DIL_WIDTH, False), (DIL_WIDTH, False),
            (DIL_WIDTH, True), (MLA_PAD, False), (MLA_PAD, False), (MLA_WIDTH, True)]
    return pl.pallas_call(
        _proj_kernel,
        name="proj",
        grid=(t // tm,),
        in_specs=([pl.BlockSpec((tm, D_MODEL), row)] + [full(a) for a in consts]
                  + [pl.BlockSpec((tm, LANE), lambda i: (i % pos_blocks, 0)) for _ in tables]
                  + [full(a) for a in gsums]),
        out_specs=[pl.BlockSpec((1, w, tm), col) if tr else pl.BlockSpec((tm, w), row) for w, tr in outs],
        out_shape=[jax.ShapeDtypeStruct((batch, w, SEQ) if tr else (t, w), BF16) for w, tr in outs],
        compiler_params=_params("parallel"),
    )(x2d, *consts, *tables, *gsums)


def _strip_kernel(src_ref, o_ref):
    chunk_w = ATT_TQ + LANE
    for a in range(STRIP_ROWS // SUBLANE):
        start = STRIP_ROWS - SUBLANE - SUBLANE * a
        lo = (start // LANE) * LANE
        chunk = src_ref[0, :, lo:lo + chunk_w]
        shift = start - lo
        if shift:
            chunk = pltpu.roll(chunk, chunk_w - shift, 1)
        o_ref[0, a * SUBLANE:(a + 1) * SUBLANE, :] = chunk[:, :ATT_TQ]


def _bias_strips(src):
    nh = src.shape[0]
    return pl.pallas_call(
        _strip_kernel,
        name="bias_strips",
        grid=(nh,),
        in_specs=[pl.BlockSpec((1, SUBLANE, STRIP_SRC_W), lambda h: (h, 0, 0))],
        out_specs=pl.BlockSpec((1, STRIP_ROWS, ATT_TQ), lambda h: (h, 0, 0)),
        out_shape=jax.ShapeDtypeStruct((nh, STRIP_ROWS, ATT_TQ), F32),
        compiler_params=_params("parallel"),
    )(src)


def _t5_bucket(rel):
    half = NUM_BUCKETS // 2
    max_exact = half // 2
    n = jnp.abs(rel)
    nf = jnp.maximum(n, 1).astype(F32)
    large = max_exact + (jnp.log(nf / max_exact) / math.log(MAX_DISTANCE / max_exact)
                         * (half - max_exact)).astype(jnp.int32)
    large = jnp.minimum(large, half - 1)
    return jnp.where(rel > 0, half, 0) + jnp.where(n < max_exact, n, large)


def _strip_sources(rel_bias):
    rel = np.arange(-(SEQ - 1), SEQ)
    mult = np.zeros(rel.shape, np.int64)
    for window, dil in DIL_PATTERNS:
        mult += (rel % dil == 0) & (np.abs(rel) // dil <= window // (2 * dil))
    table = rel_bias.astype(F32)[_t5_bucket(jnp.asarray(rel, jnp.int32))]
    logm = jnp.asarray(np.log(np.maximum(mult, 1)), F32)[:, None]
    dil_vec = jnp.where(jnp.asarray(mult > 0)[:, None], table[:, DIFF_HEADS:] + logm, NEG_BIG)
    vec = jnp.concatenate([table[:, :DIFF_HEADS], dil_vec], axis=1).T * LOG2E
    rev = jnp.pad(vec[:, ::-1], ((0, 0), (0, STRIP_SRC_W + SUBLANE - vec.shape[1])))
    return jnp.stack([rev[:, SUBLANE - 1 - s:SUBLANE - 1 - s + STRIP_SRC_W] for s in range(SUBLANE)], axis=1)


_NT = (((1,), (1,)), ((), ()))


def _lanes_between(shape, lo, hi):
    lane = lax.broadcasted_iota(jnp.int32, shape, 1)
    return (lane >= lo) & (lane < hi)


def _values_with_ones(vt, hh):
    return jnp.concatenate([vt[hh * HEAD_V:(hh + 1) * HEAD_V], jnp.ones((BF16_ROWS, vt.shape[1]), BF16)], axis=0)


def _softmax_pv_t(s_t, v_ext):
    part = jnp.max(s_t.reshape(s_t.shape[0] // MAX_SLAB, MAX_SLAB, s_t.shape[1]), axis=0)
    m = jnp.max(part, axis=0, keepdims=True)
    e = jnp.exp2(s_t - m).astype(BF16)
    pv = jnp.dot(v_ext, e, preferred_element_type=F32)
    return pv[:HEAD_V] * (1.0 / pv[HEAD_V:HEAD_V + 1])


def _pipelined(n, logits_fn, consume_fn):
    outs = []
    s_next = logits_fn(0)
    for i in range(n):
        s_cur = s_next
        if i + 1 < n:
            s_next = logits_fn(i + 1)
        outs.append(consume_fn(i, s_cur))
    return outs


def _strip_window(strip_ref, hh):
    off = pl.multiple_of((N_QBLK - 1 - pl.program_id(2)) * ATT_TQ, ATT_TQ)
    return strip_ref[hh, pl.ds(off, SEQ), :]


def _diff_kernel(q_ref, k_ref, vt_ref, strip_ref, lam_ref, g_ref, o_ref, *, lambda_init):
    q, k, vt = q_ref[0], k_ref[0], vt_ref[0]
    lp = lam_ref[...]
    lam = (jnp.exp(jnp.sum(lp[0:1] * lp[1:2], axis=-1, keepdims=True))
           - jnp.exp(jnp.sum(lp[2:3] * lp[3:4], axis=-1, keepdims=True)) + lambda_init)

    def logits(i):
        lo = DIFF_QK_DIM * i
        qm = jnp.where(_lanes_between(q.shape, lo, lo + DIFF_QK_DIM), q, jnp.zeros_like(q))
        return lax.dot_general(k, qm, _NT, preferred_element_type=F32) + _strip_window(strip_ref, i // 2)

    parts = _pipelined(4, logits, lambda i, s_t: _softmax_pv_t(s_t, _values_with_ones(vt, i // 2)))
    outs = []
    for hh in range(2):
        comb = parts[2 * hh] - lam * parts[2 * hh + 1]
        ss = jnp.sum(comb * comb, axis=0, keepdims=True)
        outs.append(comb * lax.rsqrt(ss * (1.0 / DIFF_V_DIM) + EPS))
    y = jnp.concatenate(outs, axis=0).T * g_ref[...] * (1.0 - lambda_init)
    o_ref[0] = y.astype(BF16)


def _dil_kernel(q_ref, k_ref, vt_ref, strip_ref, o_ref):
    q, k, vt = q_ref[0], k_ref[0], vt_ref[0]

    def logits(hh):
        qm = jnp.where(_lanes_between(q.shape, DIL_HEAD_DIM * hh, DIL_HEAD_DIM * (hh + 1)), q, jnp.zeros_like(q))
        return lax.dot_general(k, qm, _NT, preferred_element_type=F32) + _strip_window(strip_ref, hh)

    outs = _pipelined(2, logits, lambda hh, s_t: _softmax_pv_t(s_t, _values_with_ones(vt, hh)))
    o_ref[0] = jnp.concatenate(outs, axis=0).T.astype(BF16)


def _mla_kernel(q_ref, k_ref, vt_ref, o_ref):
    q, k, vt = q_ref[0], k_ref[0], vt_ref[0]

    def logits(hh):
        return lax.dot_general(k[:, hh * LANE:(hh + 1) * LANE], q[:, hh * LANE:(hh + 1) * LANE], _NT,
                               preferred_element_type=F32)

    outs = _pipelined(2, logits, lambda hh, s_t: _softmax_pv_t(s_t, _values_with_ones(vt, hh)))
    o_ref[0] = jnp.concatenate(outs, axis=0).T.astype(BF16)


def _attention(kernel, name, q, k, vt, qk_lanes, strips=None, head0=0, extra=()):
    b, s, _ = q.shape
    n_pairs = vt.shape[1] // LANE
    in_specs = [
        pl.BlockSpec((1, ATT_TQ, qk_lanes), lambda p, bb, i: (bb, i, p)),
        pl.BlockSpec((1, s, qk_lanes), lambda p, bb, i: (bb, 0, p)),
        pl.BlockSpec((1, LANE, s), lambda p, bb, i: (bb, p, 0)),
    ]
    args = [q, k, vt]
    if strips is not None:
        in_specs.append(pl.BlockSpec((2, STRIP_ROWS, ATT_TQ), lambda p, bb, i: (head0 // 2 + p, 0, 0),
                                     pipeline_mode=pl.Buffered(1)))
        args.append(strips)
    for a in extra:
        per_pair = a.shape[-1] == n_pairs * LANE
        in_specs.append(pl.BlockSpec((1, LANE), lambda p, bb, i: (0, p)) if per_pair
                        else pl.BlockSpec(a.shape, lambda p, bb, i: (0, 0)))
        args.append(a)
    return pl.pallas_call(
        kernel,
        name=name,
        grid=(n_pairs, b, N_QBLK),
        in_specs=in_specs,
        out_specs=pl.BlockSpec((1, ATT_TQ, LANE), lambda p, bb, i: (bb, i, p)),
        out_shape=jax.ShapeDtypeStruct((b, s, n_pairs * LANE), BF16),
        compiler_params=_params("parallel", "parallel", "parallel"),
    )(*args)


def _out_kernel(x_ref, a_ref, b_ref, c_ref, wa_ref, wb_ref, wc_ref, o_ref):
    mix = jnp.dot(a_ref[...], wa_ref[...], preferred_element_type=F32)
    mix += jnp.dot(b_ref[...], wb_ref[...], preferred_element_type=F32)
    mix += jnp.dot(c_ref[...], wc_ref[...], preferred_element_type=F32)
    o_ref[...] = x_ref[...] + mix


def _out_proj(x2d, a, b, c, wo):
    t = x2d.shape[0]
    tm = OUT_TM
    row = lambda i: (i, 0)
    fixed = lambda i: (0, 0)
    wa, wb, wc = wo[:DIFF_WIDTH], wo[DIFF_WIDTH:DIFF_WIDTH + DIL_WIDTH], wo[DIFF_WIDTH + DIL_WIDTH:]
    return pl.pallas_call(
        _out_kernel,
        name="out_proj",
        grid=(t // tm,),
        in_specs=[pl.BlockSpec((tm, D_MODEL), row),
                  pl.BlockSpec((tm, DIFF_WIDTH), row), pl.BlockSpec((tm, DIL_WIDTH), row),
                  pl.BlockSpec((tm, MLA_WIDTH), row),
                  pl.BlockSpec(wa.shape, fixed), pl.BlockSpec(wb.shape, fixed), pl.BlockSpec(wc.shape, fixed)],
        out_specs=pl.BlockSpec((tm, D_MODEL), row),
        out_shape=jax.ShapeDtypeStruct((t, D_MODEL), F32),
        compiler_params=_params("parallel"),
    )(x2d, a, b, c, wa, wb, wc)


def _rope_tables():
    half = MLA_ROPE_DIM // 2
    inv = ROPE_THETA ** (-jnp.arange(half, dtype=F32) / half)
    ang = jnp.arange(SEQ).astype(F32)[:, None] * inv[None, :]
    cos, sin = jnp.cos(ang), jnp.sin(ang)
    ones = jnp.ones((SEQ, MLA_NOPE_DIM), F32)
    tail = LANE - MLA_QK_DIM
    cos_t = jnp.concatenate([ones, cos, cos, jnp.ones((SEQ, tail), F32)], axis=1)
    sin_t = jnp.concatenate([0.0 * ones, -sin, sin, jnp.zeros((SEQ, tail), F32)], axis=1)
    return cos_t, sin_t


def _proj_weights(l, mix_norm, w_in, diff_q_norm, diff_k_norm, dil_q_norm, dil_k_norm,
                  mla_q_norm, mla_q_up, mla_kv_norm, mla_kv_up, mla_qn, mla_kn, cos_t, sin_t):
    wi = w_in[l]
    pad_to_block = LANE - MLA_QK_DIM
    w_ext = jnp.concatenate([wi[:, :SEG_ROPE], jnp.zeros((D_MODEL, MLA_NOPE_DIM), F32),
                             wi[:, SEG_ROPE:], jnp.zeros((D_MODEL, pad_to_block), F32)], axis=1)
    q_up = jnp.pad(mla_q_up[l].reshape(MLA_Q_RANK, MLA_HEADS, MLA_QK_DIM), ((0, 0), (0, 0), (0, pad_to_block)))
    kv = mla_kv_up[l].reshape(MLA_KV_RANK, MLA_HEADS, MLA_NOPE_DIM + MLA_V_DIM)
    kv_k = jnp.pad(kv[:, :, :MLA_NOPE_DIM], ((0, 0), (0, 0), (0, LANE - MLA_NOPE_DIM)))
    kv_v = kv[:, :, MLA_NOPE_DIM:]

    def head_gain(g, heads, scale=1.0, pad=0):
        return jnp.tile(jnp.pad(g.astype(F32) * scale, (0, pad)), heads)[None, :]

    return {
        "g": mix_norm[l][None, :],
        "w_in": w_ext.astype(BF16),
        "q_up": q_up.reshape(MLA_Q_RANK, MLA_PAD).astype(BF16),
        "kv_k": kv_k.reshape(MLA_KV_RANK, MLA_PAD).astype(BF16),
        "kv_v": kv_v.reshape(MLA_KV_RANK, MLA_WIDTH).astype(BF16),
        "gdq": head_gain(diff_q_norm[l], 2 * DIFF_HEADS, DIFF_QK_DIM ** -0.5 * LOG2E),
        "gdk": head_gain(diff_k_norm[l], 2 * DIFF_HEADS),
        "glq": head_gain(dil_q_norm[l], DIL_HEADS, DIL_HEAD_DIM ** -0.5 * LOG2E),
        "glk": head_gain(dil_k_norm[l], DIL_HEADS),
        "gmq": mla_q_norm[l][None, :],
        "gmkv": mla_kv_norm[l][None, :],
        "gqn": head_gain(mla_qn[l], MLA_HEADS, MLA_QK_DIM ** -0.5 * LOG2E, pad_to_block),
        "gkn": head_gain(mla_kn[l], MLA_HEADS, 1.0, pad_to_block),
        "cos": cos_t,
        "sin": sin_t,
    }


def kernel(x, rel_bias, ffn1_norm, ffn1_wg, ffn1_wu, ffn1_wd, mix_norm, w_in, diff_q_norm, diff_k_norm, diff_lambda, diff_subln, dil_q_norm, dil_k_norm, mla_q_norm, mla_q_up, mla_kv_norm, mla_kv_up, mla_qn, mla_kn, w_o, ffn2_norm, ffn2_wg, ffn2_wu, ffn2_wd):
    b, s, d = x.shape
    assert (s, d) == (SEQ, D_MODEL)
    t = b * s
    strips = _bias_strips(_strip_sources(rel_bias))
    cos_t, sin_t = _rope_tables()
    x2d = x.reshape(t, d)
    for l in range(DEPTH):
        lambda_init = 0.8 - 0.6 * math.exp(-0.3 * (l + 1))
        x2d = _ffn(x2d, ffn1_norm[l][None, :], ffn1_wg[l].astype(BF16), ffn1_wu[l].astype(BF16),
                   ffn1_wd[l].astype(BF16))
        pw = _proj_weights(l, mix_norm, w_in, diff_q_norm, diff_k_norm, dil_q_norm, dil_k_norm,
                           mla_q_norm, mla_q_up, mla_kv_norm, mla_kv_up, mla_qn, mla_kn, cos_t, sin_t)
        dq, dk, dvt, lq, lk, lvt, mq, mk, mvt = _proj(x2d, pw, b)
        tok = lambda a: a.reshape(b, s, -1)
        subln = jnp.tile(diff_subln[l].astype(F32), DIFF_HEADS)[None, :]
        out_a = _attention(functools.partial(_diff_kernel, lambda_init=lambda_init), "diff_attn",
                           tok(dq), tok(dk), dvt, LANE, strips, 0, extra=(diff_lambda[l].astype(F32), subln))
        out_b = _attention(_dil_kernel, "dil_attn", tok(lq), tok(lk), lvt, LANE, strips, DIFF_HEADS)
        out_c = _attention(_mla_kernel, "mla_attn", tok(mq), tok(mk), mvt, 2 * LANE)
        x2d = _out_proj(x2d, out_a.reshape(t, -1), out_b.reshape(t, -1), out_c.reshape(t, -1),
                        w_o[l].astype(BF16))
        x2d = _ffn(x2d, ffn2_norm[l][None, :], ffn2_wg[l].astype(BF16), ffn2_wu[l].astype(BF16),
                   ffn2_wd[l].astype(BF16))
    return x2d.reshape(b, s, d)
```

```python
import functools
import math

import jax
import jax.numpy as jnp
import numpy as np
from jax import lax
from jax.experimental import pallas as pl
from jax.experimental.pallas import tpu as pltpu

F32 = jnp.float32
BF16 = jnp.bfloat16

D_MODEL = 1024
SEQ = 2048
DEPTH = 2
EPS = 1e-6
D_FF = 2816
NUM_BUCKETS = 32
MAX_DISTANCE = 1024
ROPE_THETA = 10000.0
NEG_BIG = -1e30
LOG2E = math.log2(math.e)
DIFF_HEADS, DIFF_QK_DIM, DIFF_V_DIM = 4, 32, 64
DIL_HEADS, DIL_HEAD_DIM = 6, 64
DIL_PATTERNS = ((128, 1), (512, 4), (2048, 16))
MLA_HEADS, MLA_Q_RANK, MLA_KV_RANK = 6, 256, 128
MLA_NOPE_DIM, MLA_ROPE_DIM, MLA_V_DIM = 64, 32, 64
MLA_QK_DIM = MLA_NOPE_DIM + MLA_ROPE_DIM
HEAD_V = 64
DIFF_WIDTH = DIFF_HEADS * DIFF_V_DIM
DIL_WIDTH = DIL_HEADS * DIL_HEAD_DIM
MLA_WIDTH = MLA_HEADS * MLA_V_DIM
BIAS_HEADS = DIFF_HEADS + DIL_HEADS

LANE = 128
SUBLANE = 8
BF16_ROWS = 16
VMEM_LIMIT_BYTES = 56 * 1024 * 1024

FFN_TM = 512
PROJ_TM = 512
ATT_TQ = 512
ATT_TK = 512
MAX_SLAB = 128
STRIP_ROWS = 2 * SEQ - ATT_TQ
STRIP_SRC_W = 2 * SEQ + LANE
N_QBLK = SEQ // ATT_TQ
Q_PER_STEP = 2

SEG_DQ, SEG_DK, SEG_DV = 0, 256, 512
SEG_LQ, SEG_LK, SEG_LV = 768, 1152, 1536
SEG_MQ, SEG_MKV, SEG_ROPE = 1920, 2176, 2304
IN_EXT = 2432
MLA_PAD = MLA_HEADS * LANE


def _params(*sem):
    return pltpu.CompilerParams(dimension_semantics=sem, vmem_limit_bytes=VMEM_LIMIT_BYTES)


def _rms(x, g):
    return x * lax.rsqrt(jnp.mean(x * x, axis=-1, keepdims=True) + EPS) * g


def _ffn_kernel(*refs, fused_out):
    if fused_out:
        x_ref, a_ref, b_ref, c_ref, wa_ref, wb_ref, wc_ref, g_ref, wg_ref, wu_ref, wd_ref, o_ref = refs
        x = x_ref[...] + jnp.dot(a_ref[...], wa_ref[...], preferred_element_type=F32)
        x += jnp.dot(b_ref[...], wb_ref[...], preferred_element_type=F32)
        x += jnp.dot(c_ref[...], wc_ref[...], preferred_element_type=F32)
    else:
        x_ref, g_ref, wg_ref, wu_ref, wd_ref, o_ref = refs
        x = x_ref[...]
    n = _rms(x, g_ref[...]).astype(BF16)
    gate = jnp.dot(n, wg_ref[...], preferred_element_type=F32)
    up = jnp.dot(n, wu_ref[...], preferred_element_type=F32)
    h = (gate * jax.nn.sigmoid(gate) * up).astype(BF16)
    o_ref[...] = x + 0.5 * jnp.dot(h, wd_ref[...], preferred_element_type=F32)


def _ffn(x2d, g, wg, wu, wd, mix=None):
    t = x2d.shape[0]
    tm = FFN_TM
    row = lambda i: (i, 0)

    def resident(a):
        return pl.BlockSpec(a.shape, lambda i: (0, 0), pipeline_mode=pl.Buffered(1))

    args, in_specs = [x2d], [pl.BlockSpec((tm, D_MODEL), row)]
    if mix is not None:
        a, b, c, wo = mix
        w_parts = [wo[:DIFF_WIDTH], wo[DIFF_WIDTH:DIFF_WIDTH + DIL_WIDTH], wo[DIFF_WIDTH + DIL_WIDTH:]]
        args += [a, b, c] + w_parts
        in_specs += [pl.BlockSpec((tm, m.shape[1]), row) for m in (a, b, c)] + [resident(w) for w in w_parts]
    weights = [g, wg, wu, wd]
    return pl.pallas_call(
        functools.partial(_ffn_kernel, fused_out=mix is not None),
        name="ffn_out" if mix is not None else "ffn",
        grid=(t // tm,),
        in_specs=in_specs + [resident(w) for w in weights],
        out_specs=pl.BlockSpec((tm, D_MODEL), row),
        out_shape=jax.ShapeDtypeStruct((t, D_MODEL), F32),
        compiler_params=_params("parallel"),
    )(*args, *weights)


def _group_norm(x, gsum, inv_n, gain):
    ss = jnp.dot((x * x).astype(BF16), gsum, preferred_element_type=F32)
    return x * lax.rsqrt(ss * inv_n + EPS) * gain


def _group_norm_chunks(x, gsum, inv_n, gain):
    w = gsum.shape[0]
    parts = [_group_norm(x[:, c:c + w], gsum, inv_n, gain[:, c:c + w]) for c in range(0, x.shape[1], w)]
    return jnp.concatenate(parts, axis=1)


def _rope_blocks(x, cos, sin):
    lane = lax.broadcasted_iota(jnp.int32, (x.shape[0], LANE), 1)
    first_half = lane < MLA_NOPE_DIM + MLA_ROPE_DIM // 2
    parts = []
    for c in range(0, x.shape[1], LANE):
        xb = x[:, c:c + LANE]
        partner = jnp.where(first_half,
                            pltpu.roll(xb, LANE - MLA_ROPE_DIM // 2, 1),
                            pltpu.roll(xb, MLA_ROPE_DIM // 2, 1))
        parts.append(xb * cos + partner * sin)
    return jnp.concatenate(parts, axis=1)


def _proj_kernel(x_ref, g_ref, win_ref, qup_ref, kvk_ref, kvv_ref,
                 gdq_ref, gdk_ref, glq_ref, glk_ref, gmq_ref, gmkv_ref, gqn_ref, gkn_ref,
                 cos_ref, sin_ref, g32_ref, g64_ref, g128_ref,
                 dq_ref, dk_ref, dvt_ref, lq_ref, lk_ref, lvt_ref, mq_ref, mk_ref, mvt_ref):
    h = _rms(x_ref[...], g_ref[...]).astype(BF16)
    proj = jnp.dot(h, win_ref[...], preferred_element_type=F32)

    def seg(lo, hi):
        return proj[:, lo:hi]

    g32, g64, g128 = g32_ref[...], g64_ref[...], g128_ref[...]
    dq_ref[...] = _group_norm(seg(SEG_DQ, SEG_DK), g32, 1.0 / DIFF_QK_DIM, gdq_ref[...]).astype(BF16)
    dk_ref[...] = _group_norm(seg(SEG_DK, SEG_DV), g32, 1.0 / DIFF_QK_DIM, gdk_ref[...]).astype(BF16)
    dvt_ref[0] = seg(SEG_DV, SEG_LQ).T.astype(BF16)
    lqk = _group_norm_chunks(seg(SEG_LQ, SEG_LV), g64, 1.0 / DIL_HEAD_DIM,
                             jnp.concatenate([glq_ref[...], glk_ref[...]], axis=1))
    lq_ref[...] = lqk[:, :DIL_WIDTH].astype(BF16)
    lk_ref[...] = lqk[:, DIL_WIDTH:].astype(BF16)
    lvt_ref[0] = seg(SEG_LV, SEG_MQ).T.astype(BF16)

    cos, sin = cos_ref[...], sin_ref[...]
    q_lat = _rms(seg(SEG_MQ, SEG_MKV), gmq_ref[...]).astype(BF16)
    q = jnp.dot(q_lat, qup_ref[...], preferred_element_type=F32)
    q = _group_norm_chunks(q, g128, 1.0 / MLA_QK_DIM, gqn_ref[...])
    mq_ref[...] = _rope_blocks(q, cos, sin).astype(BF16)

    c_kv = _rms(seg(SEG_MKV, SEG_ROPE), gmkv_ref[...]).astype(BF16)
    k_rope = seg(SEG_ROPE, IN_EXT)
    k = jnp.dot(c_kv, kvk_ref[...], preferred_element_type=F32)
    k = k + jnp.concatenate([k_rope] * MLA_HEADS, axis=1)
    k = _group_norm_chunks(k, g128, 1.0 / MLA_QK_DIM, gkn_ref[...])
    mk_ref[...] = _rope_blocks(k, cos, sin).astype(BF16)
    mvt_ref[0] = jnp.dot(c_kv, kvv_ref[...], preferred_element_type=F32).T.astype(BF16)


def _block_diag_ones(width, group):
    idx = np.arange(width) // group
    return jnp.asarray((idx[:, None] == idx[None, :]).astype(np.float32), dtype=BF16)


def _proj(x2d, pw, batch):
    t = x2d.shape[0]
    tm = PROJ_TM
    row = lambda i: (i, 0)
    fixed = lambda i: (0, 0)
    pos_blocks = SEQ // tm
    col = lambda i: (i // pos_blocks, 0, i % pos_blocks)

    def full(a):
        return pl.BlockSpec(a.shape, fixed)

    consts = [pw["g"], pw["w_in"], pw["q_up"], pw["kv_k"], pw["kv_v"],
              pw["gdq"], pw["gdk"], pw["glq"], pw["glk"], pw["gmq"], pw["gmkv"], pw["gqn"], pw["gkn"]]
    tables = [pw["cos"], pw["sin"]]
    gsums = [_block_diag_ones(256, DIFF_QK_DIM), _block_diag_ones(256, DIL_HEAD_DIM),
             _block_diag_ones(256, LANE)]
    outs = [(256, False), (256, False), (DIFF_WIDTH, True), (DIL_WIDTH, False), (DIL_WIDTH, False),
            (DIL_WIDTH, True), (MLA_PAD, False), (MLA_PAD, False), (MLA_WIDTH, True)]
    return pl.pallas_call(
        _proj_kernel,
        name="proj",
        grid=(t // tm,),
        in_specs=([pl.BlockSpec((tm, D_MODEL), row)] + [full(a) for a in consts]
                  + [pl.BlockSpec((tm, LANE), lambda i: (i % pos_blocks, 0)) for _ in tables]
                  + [full(a) for a in gsums]),
        out_specs=[pl.BlockSpec((1, w, tm), col) if tr else pl.BlockSpec((tm, w), row) for w, tr in outs],
        out_shape=[jax.ShapeDtypeStruct((batch, w, SEQ) if tr else (t, w), BF16) for w, tr in outs],
        compiler_params=_params("parallel"),
    )(x2d, *consts, *tables, *gsums)


def _strip_kernel(src_ref, o_ref):
    chunk_w = ATT_TQ + LANE
    for a in range(STRIP_ROWS // SUBLANE):
        start = STRIP_ROWS - SUBLANE - SUBLANE * a
        lo = (start // LANE) * LANE
        chunk = src_ref[0, :, lo:lo + chunk_w]
        shift = start - lo
        if shift:
            chunk = pltpu.roll(chunk, chunk_w - shift, 1)
        o_ref[0, a * SUBLANE:(a + 1) * SUBLANE, :] = chunk[:, :ATT_TQ]


def _bias_strips(src):
    nh = src.shape[0]
    return pl.pallas_call(
        _strip_kernel,
        name="bias_strips",
        grid=(nh,),
        in_specs=[pl.BlockSpec((1, SUBLANE, STRIP_SRC_W), lambda h: (h, 0, 0))],
        out_specs=pl.BlockSpec((1, STRIP_ROWS, ATT_TQ), lambda h: (h, 0, 0)),
        out_shape=jax.ShapeDtypeStruct((nh, STRIP_ROWS, ATT_TQ), F32),
        compiler_params=_params("parallel"),
    )(src)


def _t5_bucket(rel):
    half = NUM_BUCKETS // 2
    max_exact = half // 2
    n = jnp.abs(rel)
    nf = jnp.maximum(n, 1).astype(F32)
    large = max_exact + (jnp.log(nf / max_exact) / math.log(MAX_DISTANCE / max_exact)
                         * (half - max_exact)).astype(jnp.int32)
    large = jnp.minimum(large, half - 1)
    return jnp.where(rel > 0, half, 0) + jnp.where(n < max_exact, n, large)


def _strip_sources(rel_bias):
    rel = np.arange(-(SEQ - 1), SEQ)
    mult = np.zeros(rel.shape, np.int64)
    for window, dil in DIL_PATTERNS:
        mult += (rel % dil == 0) & (np.abs(rel) // dil <= window // (2 * dil))
    table = rel_bias.astype(F32)[_t5_bucket(jnp.asarray(rel, jnp.int32))]
    logm = jnp.asarray(np.log(np.maximum(mult, 1)), F32)[:, None]
    dil_vec = jnp.where(jnp.asarray(mult > 0)[:, None], table[:, DIFF_HEADS:] + logm, NEG_BIG)
    vec = jnp.concatenate([table[:, :DIFF_HEADS], dil_vec], axis=1).T * LOG2E
    rev = jnp.pad(vec[:, ::-1], ((0, 0), (0, STRIP_SRC_W + SUBLANE - vec.shape[1])))
    return jnp.stack([rev[:, SUBLANE - 1 - s:SUBLANE - 1 - s + STRIP_SRC_W] for s in range(SUBLANE)], axis=1)


_NT = (((1,), (1,)), ((), ()))


def _lanes_between(shape, lo, hi):
    lane = lax.broadcasted_iota(jnp.int32, shape, 1)
    return (lane >= lo) & (lane < hi)


def _values_with_ones(vt, hh):
    return jnp.concatenate([vt[hh * HEAD_V:(hh + 1) * HEAD_V], jnp.ones((BF16_ROWS, vt.shape[1]), BF16)], axis=0)


def _col_max(s_t):
    part = jnp.max(s_t.reshape(s_t.shape[0] // MAX_SLAB, MAX_SLAB, s_t.shape[1]), axis=0)
    return jnp.max(part, axis=0, keepdims=True)


def _attend(n_items, logits_fn, values_fn):
    n_chunks = SEQ // ATT_TK
    total = n_items * n_chunks
    outs = []
    s_next = logits_fn(0, 0)
    m = acc = None
    for idx in range(total):
        item, c = divmod(idx, n_chunks)
        s_cur = s_next
        if idx + 1 < total:
            s_next = logits_fn(*divmod(idx + 1, n_chunks))
        c_max = _col_max(s_cur)
        m_new = c_max if c == 0 else jnp.maximum(m, c_max)
        e = jnp.exp2((s_cur - m_new).astype(BF16))
        pv = jnp.dot(values_fn(item)[:, c * ATT_TK:(c + 1) * ATT_TK], e, preferred_element_type=F32)
        acc = pv if c == 0 else acc * jnp.exp2(m - m_new) + pv
        m = m_new
        if c == n_chunks - 1:
            outs.append(acc[:HEAD_V] * (1.0 / acc[HEAD_V:HEAD_V + 1]))
    return outs


def _q_rows(j):
    return slice(j * ATT_TQ, (j + 1) * ATT_TQ)


def _k_rows(c):
    return slice(c * ATT_TK, (c + 1) * ATT_TK)


def _strip_window(strip_ref, hh, j, c):
    qblk = pl.program_id(2) * Q_PER_STEP + j
    off = pl.multiple_of((N_QBLK - 1 - qblk) * ATT_TQ + c * ATT_TK, math.gcd(ATT_TQ, ATT_TK))
    return strip_ref[hh, pl.ds(off, ATT_TK), :]


def _masked_q(q_ref, j, lo, hi):
    q = q_ref[0, _q_rows(j), :]
    return jnp.where(_lanes_between(q.shape, lo, hi), q, jnp.zeros_like(q))


def _diff_kernel(q_ref, k_ref, vt_ref, strip_ref, lam_ref, g_ref, o_ref, *, lambda_init):
    vt = vt_ref[0]
    lp = lam_ref[...]
    lam = (jnp.exp(jnp.sum(lp[0:1] * lp[1:2], axis=-1, keepdims=True))
           - jnp.exp(jnp.sum(lp[2:3] * lp[3:4], axis=-1, keepdims=True)) + lambda_init)
    subs = 2 * 2
    v_ext = [_values_with_ones(vt, hh) for hh in range(2)]

    def logits(i, c):
        j, sub = divmod(i, subs)
        qm = _masked_q(q_ref, j, DIFF_QK_DIM * sub, DIFF_QK_DIM * (sub + 1))
        return (lax.dot_general(k_ref[0, _k_rows(c), :], qm, _NT, preferred_element_type=F32)
                + _strip_window(strip_ref, sub // 2, j, c))

    parts = _attend(Q_PER_STEP * subs, logits, lambda i: v_ext[(i % subs) // 2])
    for j in range(Q_PER_STEP):
        outs = []
        for hh in range(2):
            comb = parts[j * subs + 2 * hh] - lam * parts[j * subs + 2 * hh + 1]
            ss = jnp.sum(comb * comb, axis=0, keepdims=True)
            outs.append(comb * lax.rsqrt(ss * (1.0 / DIFF_V_DIM) + EPS))
        y = jnp.concatenate(outs, axis=0).T * g_ref[...] * (1.0 - lambda_init)
        o_ref[0, _q_rows(j), :] = y.astype(BF16)


def _dil_kernel(q_ref, k_ref, vt_ref, strip_ref, o_ref):
    vt = vt_ref[0]
    v_ext = [_values_with_ones(vt, hh) for hh in range(2)]

    def logits(i, c):
        j, hh = divmod(i, 2)
        qm = _masked_q(q_ref, j, DIL_HEAD_DIM * hh, DIL_HEAD_DIM * (hh + 1))
        return (lax.dot_general(k_ref[0, _k_rows(c), :], qm, _NT, preferred_element_type=F32)
                + _strip_window(strip_ref, hh, j, c))

    outs = _attend(Q_PER_STEP * 2, logits, lambda i: v_ext[i % 2])
    for j in range(Q_PER_STEP):
        o_ref[0, _q_rows(j), :] = jnp.concatenate(outs[2 * j:2 * j + 2], axis=0).T.astype(BF16)


def _mla_kernel(q_ref, k_ref, vt_ref, o_ref):
    vt = vt_ref[0]
    v_ext = [_values_with_ones(vt, hh) for hh in range(2)]

    def logits(i, c):
        j, hh = divmod(i, 2)
        return lax.dot_general(k_ref[0, _k_rows(c), hh * LANE:(hh + 1) * LANE],
                               q_ref[0, _q_rows(j), hh * LANE:(hh + 1) * LANE], _NT,
                               preferred_element_type=F32)

    outs = _attend(Q_PER_STEP * 2, logits, lambda i: v_ext[i % 2])
    for j in range(Q_PER_STEP):
        o_ref[0, _q_rows(j), :] = jnp.concatenate(outs[2 * j:2 * j + 2], axis=0).T.astype(BF16)


def _attention(kernel, name, q, k, vt, qk_lanes, strips=None, head0=0, extra=()):
    b, s, _ = q.shape
    n_pairs = vt.shape[1] // LANE
    tq = Q_PER_STEP * ATT_TQ
    in_specs = [
        pl.BlockSpec((1, tq, qk_lanes), lambda p, bb, i: (bb, i, p)),
        pl.BlockSpec((1, s, qk_lanes), lambda p, bb, i: (bb, 0, p)),
        pl.BlockSpec((1, LANE, s), lambda p, bb, i: (bb, p, 0)),
    ]
    args = [q, k, vt]
    if strips is not None:
        in_specs.append(pl.BlockSpec((2, STRIP_ROWS, ATT_TQ), lambda p, bb, i: (head0 // 2 + p, 0, 0),
                                     pipeline_mode=pl.Buffered(1)))
        args.append(strips)
    for a in extra:
        per_pair = a.shape[-1] == n_pairs * LANE
        in_specs.append(pl.BlockSpec((1, LANE), lambda p, bb, i: (0, p)) if per_pair
                        else pl.BlockSpec(a.shape, lambda p, bb, i: (0, 0)))
        args.append(a)
    return pl.pallas_call(
        kernel,
        name=name,
        grid=(n_pairs, b, s // tq),
        in_specs=in_specs,
        out_specs=pl.BlockSpec((1, tq, LANE), lambda p, bb, i: (bb, i, p)),
        out_shape=jax.ShapeDtypeStruct((b, s, n_pairs * LANE), BF16),
        compiler_params=_params("parallel", "parallel", "parallel"),
    )(*args)


def _rope_tables():
    half = MLA_ROPE_DIM // 2
    inv = ROPE_THETA ** (-jnp.arange(half, dtype=F32) / half)
    ang = jnp.arange(SEQ).astype(F32)[:, None] * inv[None, :]
    cos, sin = jnp.cos(ang), jnp.sin(ang)
    ones = jnp.ones((SEQ, MLA_NOPE_DIM), F32)
    tail = LANE - MLA_QK_DIM
    cos_t = jnp.concatenate([ones, cos, cos, jnp.ones((SEQ, tail), F32)], axis=1)
    sin_t = jnp.concatenate([0.0 * ones, -sin, sin, jnp.zeros((SEQ, tail), F32)], axis=1)
    return cos_t, sin_t


def _proj_weights(l, mix_norm, w_in, diff_q_norm, diff_k_norm, dil_q_norm, dil_k_norm,
                  mla_q_norm, mla_q_up, mla_kv_norm, mla_kv_up, mla_qn, mla_kn, cos_t, sin_t):
    wi = w_in[l]
    pad_to_block = LANE - MLA_QK_DIM
    w_ext = jnp.concatenate([wi[:, :SEG_ROPE], jnp.zeros((D_MODEL, MLA_NOPE_DIM), F32),
                             wi[:, SEG_ROPE:], jnp.zeros((D_MODEL, pad_to_block), F32)], axis=1)
    q_up = jnp.pad(mla_q_up[l].reshape(MLA_Q_RANK, MLA_HEADS, MLA_QK_DIM), ((0, 0), (0, 0), (0, pad_to_block)))
    kv = mla_kv_up[l].reshape(MLA_KV_RANK, MLA_HEADS, MLA_NOPE_DIM + MLA_V_DIM)
    kv_k = jnp.pad(kv[:, :, :MLA_NOPE_DIM], ((0, 0), (0, 0), (0, LANE - MLA_NOPE_DIM)))
    kv_v = kv[:, :, MLA_NOPE_DIM:]

    def head_gain(g, heads, scale=1.0, pad=0):
        return jnp.tile(jnp.pad(g.astype(F32) * scale, (0, pad)), heads)[None, :]

    return {
        "g": mix_norm[l][None, :],
        "w_in": w_ext.astype(BF16),
        "q_up": q_up.reshape(MLA_Q_RANK, MLA_PAD).astype(BF16),
        "kv_k": kv_k.reshape(MLA_KV_RANK, MLA_PAD).astype(BF16),
        "kv_v": kv_v.reshape(MLA_KV_RANK, MLA_WIDTH).astype(BF16),
        "gdq": head_gain(diff_q_norm[l], 2 * DIFF_HEADS, DIFF_QK_DIM ** -0.5 * LOG2E),
        "gdk": head_gain(diff_k_norm[l], 2 * DIFF_HEADS),
        "glq": head_gain(dil_q_norm[l], DIL_HEADS, DIL_HEAD_DIM ** -0.5 * LOG2E),
        "glk": head_gain(dil_k_norm[l], DIL_HEADS),
        "gmq": mla_q_norm[l][None, :],
        "gmkv": mla_kv_norm[l][None, :],
        "gqn": head_gain(mla_qn[l], MLA_HEADS, MLA_QK_DIM ** -0.5 * LOG2E, pad_to_block),
        "gkn": head_gain(mla_kn[l], MLA_HEADS, 1.0, pad_to_block),
        "cos": cos_t,
        "sin": sin_t,
    }


def kernel(x, rel_bias, ffn1_norm, ffn1_wg, ffn1_wu, ffn1_wd, mix_norm, w_in, diff_q_norm, diff_k_norm, diff_lambda, diff_subln, dil_q_norm, dil_k_norm, mla_q_norm, mla_q_up, mla_kv_norm, mla_kv_up, mla_qn, mla_kn, w_o, ffn2_norm, ffn2_wg, ffn2_wu, ffn2_wd):
    b, s, d = x.shape
    assert (s, d) == (SEQ, D_MODEL)
    t = b * s
    strips = _bias_strips(_strip_sources(rel_bias))
    cos_t, sin_t = _rope_tables()
    x2d = x.reshape(t, d)
    for l in range(DEPTH):
        lambda_init = 0.8 - 0.6 * math.exp(-0.3 * (l + 1))
        x2d = _ffn(x2d, ffn1_norm[l][None, :], ffn1_wg[l].astype(BF16), ffn1_wu[l].astype(BF16),
                   ffn1_wd[l].astype(BF16))
        pw = _proj_weights(l, mix_norm, w_in, diff_q_norm, diff_k_norm, dil_q_norm, dil_k_norm,
                           mla_q_norm, mla_q_up, mla_kv_norm, mla_kv_up, mla_qn, mla_kn, cos_t, sin_t)
        dq, dk, dvt, lq, lk, lvt, mq, mk, mvt = _proj(x2d, pw, b)
        tok = lambda a: a.reshape(b, s, -1)
        subln = jnp.tile(diff_subln[l].astype(F32), DIFF_HEADS)[None, :]
        out_a = _attention(functools.partial(_diff_kernel, lambda_init=lambda_init), "diff_attn",
                           tok(dq), tok(dk), dvt, LANE, strips, 0, extra=(diff_lambda[l].astype(F32), subln))
        out_b = _attention(_dil_kernel, "dil_attn", tok(lq), tok(lk), lvt, LANE, strips, DIFF_HEADS)
        out_c = _attention(_mla_kernel, "mla_attn", tok(mq), tok(mk), mvt, 2 * LANE)
        mix = (out_a.reshape(t, -1), out_b.reshape(t, -1), out_c.reshape(t, -1), w_o[l].astype(BF16))
        x2d = _ffn(x2d, ffn2_norm[l][None, :], ffn2_wg[l].astype(BF16), ffn2_wu[l].astype(BF16),
                   ffn2_wd[l].astype(BF16), mix=mix)
    return x2d.reshape(b, s, d)
```

```python
import functools
import math

import jax
import jax.numpy as jnp
import numpy as np
from jax import lax
from jax.experimental import pallas as pl
from jax.experimental.pallas import tpu as pltpu

F32 = jnp.float32
BF16 = jnp.bfloat16

D_MODEL = 1024
SEQ = 2048
DEPTH = 2
EPS = 1e-6
D_FF = 2816
NUM_BUCKETS = 32
MAX_DISTANCE = 1024
ROPE_THETA = 10000.0
NEG_BIG = -1e30
LOG2E = math.log2(math.e)
DIFF_HEADS, DIFF_QK_DIM, DIFF_V_DIM = 4, 32, 64
DIL_HEADS, DIL_HEAD_DIM = 6, 64
DIL_PATTERNS = ((128, 1), (512, 4), (2048, 16))
MLA_HEADS, MLA_Q_RANK, MLA_KV_RANK = 6, 256, 128
MLA_NOPE_DIM, MLA_ROPE_DIM, MLA_V_DIM = 64, 32, 64
MLA_QK_DIM = MLA_NOPE_DIM + MLA_ROPE_DIM
HEAD_V = 64
DIFF_WIDTH = DIFF_HEADS * DIFF_V_DIM
DIL_WIDTH = DIL_HEADS * DIL_HEAD_DIM
MLA_WIDTH = MLA_HEADS * MLA_V_DIM
BIAS_HEADS = DIFF_HEADS + DIL_HEADS

LANE = 128
SUBLANE = 8
BF16_ROWS = 16
VMEM_LIMIT_BYTES = 56 * 1024 * 1024

FFN_TM = 512
CAST_ROWS = 256
PROJ_TM = 512
ATT_TQ = 512
ATT_TK = 512
MAX_SLAB = 128
LOGIT_BOUND = 50.0
STRIP_ROWS = 2 * SEQ - ATT_TQ
STRIP_SRC_W = 2 * SEQ + LANE
N_QBLK = SEQ // ATT_TQ
Q_PER_STEP = 2

SEG_DQ, SEG_DK, SEG_DV = 0, 256, 512
SEG_LQ, SEG_LK, SEG_LV = 768, 1152, 1536
SEG_MQ, SEG_MKV, SEG_ROPE = 1920, 2176, 2304
IN_EXT = 2432
MLA_PAD = MLA_HEADS * LANE


def _params(*sem):
    return pltpu.CompilerParams(dimension_semantics=sem, vmem_limit_bytes=VMEM_LIMIT_BYTES)


def _rms(x, g):
    return x * lax.rsqrt(jnp.mean(x * x, axis=-1, keepdims=True) + EPS) * g


def _cast_kernel(w_ref, o_ref):
    o_ref[...] = w_ref[...].astype(BF16)


def _cast_stack(w):
    layers, rows, cols = w.shape
    blk = pl.BlockSpec((1, CAST_ROWS, cols), lambda l, r: (l, r, 0))
    return pl.pallas_call(
        _cast_kernel,
        name="cast_bf16",
        grid=(layers, rows // CAST_ROWS),
        in_specs=[blk],
        out_specs=blk,
        out_shape=jax.ShapeDtypeStruct(w.shape, BF16),
        compiler_params=_params("parallel", "parallel"),
    )(w)


def _ffn_kernel(*refs, fused_out):
    if fused_out:
        x_ref, a_ref, b_ref, c_ref, wo_ref, g_ref, wg_ref, wu_ref, wd_ref, o_ref = refs
        x = x_ref[...]
        row = 0
        for part in (a_ref, b_ref, c_ref):
            width = part.shape[1]
            x += jnp.dot(part[...], wo_ref[row:row + width, :], preferred_element_type=F32)
            row += width
    else:
        x_ref, g_ref, wg_ref, wu_ref, wd_ref, o_ref = refs
        x = x_ref[...]
    n = _rms(x, g_ref[...]).astype(BF16)
    gate = jnp.dot(n, wg_ref[...], preferred_element_type=F32)
    up = jnp.dot(n, wu_ref[...], preferred_element_type=F32)
    h = (gate * jax.nn.sigmoid(gate) * up).astype(BF16)
    o_ref[...] = x + 0.5 * jnp.dot(h, wd_ref[...], preferred_element_type=F32)


def _ffn(x2d, layer, g, wg, wu, wd, mix=None):
    t = x2d.shape[0]
    tm = FFN_TM
    row = lambda i: (i, 0)

    def resident(a):
        return pl.BlockSpec((None,) + a.shape[1:], lambda i: (layer, 0, 0), pipeline_mode=pl.Buffered(1))

    args, in_specs = [x2d], [pl.BlockSpec((tm, D_MODEL), row)]
    if mix is not None:
        a, b, c, wo = mix
        args += [a, b, c, wo]
        in_specs += [pl.BlockSpec((tm, m.shape[1]), row) for m in (a, b, c)] + [resident(wo)]
    weights = [g[:, None, :], wg, wu, wd]
    return pl.pallas_call(
        functools.partial(_ffn_kernel, fused_out=mix is not None),
        name="ffn_out" if mix is not None else "ffn",
        grid=(t // tm,),
        in_specs=in_specs + [resident(w) for w in weights],
        out_specs=pl.BlockSpec((tm, D_MODEL), row),
        out_shape=jax.ShapeDtypeStruct((t, D_MODEL), F32),
        compiler_params=_params("parallel"),
    )(*args, *weights)


def _group_norm(x, gsum, inv_n, gain):
    ss = jnp.dot((x * x).astype(BF16), gsum, preferred_element_type=F32)
    return x * lax.rsqrt(ss * inv_n + EPS) * gain


def _group_norm_chunks(x, gsum, inv_n, gain):
    w = gsum.shape[0]
    parts = [_group_norm(x[:, c:c + w], gsum, inv_n, gain[:, c:c + w]) for c in range(0, x.shape[1], w)]
    return jnp.concatenate(parts, axis=1)


def _rope_blocks(x, cos, sin):
    lane = lax.broadcasted_iota(jnp.int32, (x.shape[0], LANE), 1)
    first_half = lane < MLA_NOPE_DIM + MLA_ROPE_DIM // 2
    parts = []
    for c in range(0, x.shape[1], LANE):
        xb = x[:, c:c + LANE]
        partner = jnp.where(first_half,
                            pltpu.roll(xb, LANE - MLA_ROPE_DIM // 2, 1),
                            pltpu.roll(xb, MLA_ROPE_DIM // 2, 1))
        parts.append(xb * cos + partner * sin)
    return jnp.concatenate(parts, axis=1)


def _proj_kernel(x_ref, g_ref, win_ref, qup_ref, kvk_ref, kvv_ref,
                 gdq_ref, gdk_ref, glq_ref, glk_ref, gmq_ref, gmkv_ref, gqn_ref, gkn_ref,
                 cos_ref, sin_ref, g32_ref, g64_ref, g128_ref,
                 dq_ref, dk_ref, dvt_ref, lq_ref, lk_ref, lvt_ref, mq_ref, mk_ref, mvt_ref):
    h = _rms(x_ref[...], g_ref[...]).astype(BF16)
    proj = jnp.dot(h, win_ref[...], preferred_element_type=F32)

    def seg(lo, hi):
        return proj[:, lo:hi]

    g32, g64, g128 = g32_ref[...], g64_ref[...], g128_ref[...]
    dq_ref[...] = _group_norm(seg(SEG_DQ, SEG_DK), g32, 1.0 / DIFF_QK_DIM, gdq_ref[...]).astype(BF16)
    dk_ref[...] = _group_norm(seg(SEG_DK, SEG_DV), g32, 1.0 / DIFF_QK_DIM, gdk_ref[...]).astype(BF16)
    dvt_ref[0] = seg(SEG_DV, SEG_LQ).T.astype(BF16)
    lqk = _group_norm_chunks(seg(SEG_LQ, SEG_LV), g64, 1.0 / DIL_HEAD_DIM,
                             jnp.concatenate([glq_ref[...], glk_ref[...]], axis=1))
    lq_ref[...] = lqk[:, :DIL_WIDTH].astype(BF16)
    lk_ref[...] = lqk[:, DIL_WIDTH:].astype(BF16)
    lvt_ref[0] = seg(SEG_LV, SEG_MQ).T.astype(BF16)

    cos, sin = cos_ref[...], sin_ref[...]
    q_lat = _rms(seg(SEG_MQ, SEG_MKV), gmq_ref[...]).astype(BF16)
    q = jnp.dot(q_lat, qup_ref[...], preferred_element_type=F32)
    q = _group_norm_chunks(q, g128, 1.0 / MLA_QK_DIM, gqn_ref[...])
    mq_ref[...] = _rope_blocks(q, cos, sin).astype(BF16)

    c_kv = _rms(seg(SEG_MKV, SEG_ROPE), gmkv_ref[...]).astype(BF16)
    k_rope = seg(SEG_ROPE, IN_EXT)
    k = jnp.dot(c_kv, kvk_ref[...], preferred_element_type=F32)
    k = k + jnp.concatenate([k_rope] * MLA_HEADS, axis=1)
    k = _group_norm_chunks(k, g128, 1.0 / MLA_QK_DIM, gkn_ref[...])
    mk_ref[...] = _rope_blocks(k, cos, sin).astype(BF16)
    mvt_ref[0] = jnp.dot(c_kv, kvv_ref[...], preferred_element_type=F32).T.astype(BF16)


def _block_diag_ones(width, group):
    idx = np.arange(width) // group
    return jnp.asarray((idx[:, None] == idx[None, :]).astype(np.float32), dtype=BF16)


def _proj(x2d, pw, batch):
    t = x2d.shape[0]
    tm = PROJ_TM
    row = lambda i: (i, 0)
    fixed = lambda i: (0, 0)
    pos_blocks = SEQ // tm
    col = lambda i: (i // pos_blocks, 0, i % pos_blocks)

    def full(a):
        return pl.BlockSpec(a.shape, fixed)

    consts = [pw["g"], pw["w_in"], pw["q_up"], pw["kv_k"], pw["kv_v"],
              pw["gdq"], pw["gdk"], pw["glq"], pw["glk"], pw["gmq"], pw["gmkv"], pw["gqn"], pw["gkn"]]
    tables = [pw["cos"], pw["sin"]]
    gsums = [_block_diag_ones(256, DIFF_QK_DIM), _block_diag_ones(256, DIL_HEAD_DIM),
             _block_diag_ones(256, LANE)]
    outs = [(256, False), (256, False), (DIFF_WIDTH, True), (DIL_WIDTH, False), (DIL_WIDTH, False),
            (DIL_WIDTH, True), (MLA_PAD, False), (MLA_PAD, False), (MLA_WIDTH, True)]
    return pl.pallas_call(
        _proj_kernel,
        name="proj",
        grid=(t // tm,),
        in_specs=([pl.BlockSpec((tm, D_MODEL), row)] + [full(a) for a in consts]
                  + [pl.BlockSpec((tm, LANE), lambda i: (i % pos_blocks, 0)) for _ in tables]
                  + [full(a) for a in gsums]),
        out_specs=[pl.BlockSpec((1, w, tm), col) if tr else pl.BlockSpec((tm, w), row) for w, tr in outs],
        out_shape=[jax.ShapeDtypeStruct((batch, w, SEQ) if tr else (t, w), BF16) for w, tr in outs],
        compiler_params=_params("parallel"),
    )(x2d, *consts, *tables, *gsums)


def _strip_kernel(src_ref, o_ref):
    chunk_w = ATT_TQ + LANE
    for a in range(STRIP_ROWS // SUBLANE):
        start = STRIP_ROWS - SUBLANE - SUBLANE * a
        lo = (start // LANE) * LANE
        chunk = src_ref[0, :, lo:lo + chunk_w]
        shift = start - lo
        if shift:
            chunk = pltpu.roll(chunk, chunk_w - shift, 1)
        o_ref[0, a * SUBLANE:(a + 1) * SUBLANE, :] = chunk[:, :ATT_TQ]


def _bias_strips(src):
    nh = src.shape[0]
    return pl.pallas_call(
        _strip_kernel,
        name="bias_strips",
        grid=(nh,),
        in_specs=[pl.BlockSpec((1, SUBLANE, STRIP_SRC_W), lambda h: (h, 0, 0))],
        out_specs=pl.BlockSpec((1, STRIP_ROWS, ATT_TQ), lambda h: (h, 0, 0)),
        out_shape=jax.ShapeDtypeStruct((nh, STRIP_ROWS, ATT_TQ), F32),
        compiler_params=_params("parallel"),
    )(src)


def _t5_bucket(rel):
    half = NUM_BUCKETS // 2
    max_exact = half // 2
    n = jnp.abs(rel)
    nf = jnp.maximum(n, 1).astype(F32)
    large = max_exact + (jnp.log(nf / max_exact) / math.log(MAX_DISTANCE / max_exact)
                         * (half - max_exact)).astype(jnp.int32)
    large = jnp.minimum(large, half - 1)
    return jnp.where(rel > 0, half, 0) + jnp.where(n < max_exact, n, large)


def _strip_sources(rel_bias):
    rel = np.arange(-(SEQ - 1), SEQ)
    mult = np.zeros(rel.shape, np.int64)
    for window, dil in DIL_PATTERNS:
        mult += (rel % dil == 0) & (np.abs(rel) // dil <= window // (2 * dil))
    table = rel_bias.astype(F32)[_t5_bucket(jnp.asarray(rel, jnp.int32))]
    logm = jnp.asarray(np.log(np.maximum(mult, 1)), F32)[:, None]
    dil_vec = jnp.where(jnp.asarray(mult > 0)[:, None], table[:, DIFF_HEADS:] + logm, NEG_BIG)
    vec = jnp.concatenate([table[:, :DIFF_HEADS], dil_vec], axis=1).T * LOG2E
    rev = jnp.pad(vec[:, ::-1], ((0, 0), (0, STRIP_SRC_W + SUBLANE - vec.shape[1])))
    return jnp.stack([rev[:, SUBLANE - 1 - s:SUBLANE - 1 - s + STRIP_SRC_W] for s in range(SUBLANE)], axis=1)


_NT = (((1,), (1,)), ((), ()))


def _lanes_between(shape, lo, hi):
    lane = lax.broadcasted_iota(jnp.int32, shape, 1)
    return (lane >= lo) & (lane < hi)


def _values_with_ones(vt, hh):
    return jnp.concatenate([vt[hh * HEAD_V:(hh + 1) * HEAD_V], jnp.ones((BF16_ROWS, vt.shape[1]), BF16)], axis=0)


def _col_max(s_t):
    part = jnp.max(s_t.reshape(s_t.shape[0] // MAX_SLAB, MAX_SLAB, s_t.shape[1]), axis=0)
    return jnp.max(part, axis=0, keepdims=True)


def _attend(n_items, logits_fn, values_fn, bounded):
    n_chunks = SEQ // ATT_TK
    total = n_items * n_chunks
    outs = []
    s_next = logits_fn(0, 0)
    m = acc = None
    for idx in range(total):
        item, c = divmod(idx, n_chunks)
        s_cur = s_next
        if idx + 1 < total:
            s_next = logits_fn(*divmod(idx + 1, n_chunks))
        values = values_fn(item)[:, c * ATT_TK:(c + 1) * ATT_TK]
        if bounded:
            pv = jnp.dot(values, jnp.exp2(s_cur).astype(BF16), preferred_element_type=F32)
            acc = pv if c == 0 else acc + pv
        else:
            c_max = _col_max(s_cur)
            m_new = c_max if c == 0 else jnp.maximum(m, c_max)
            pv = jnp.dot(values, jnp.exp2((s_cur - m_new).astype(BF16)), preferred_element_type=F32)
            acc = pv if c == 0 else acc * jnp.exp2(m - m_new) + pv
            m = m_new
        if c == n_chunks - 1:
            outs.append(acc[:HEAD_V] * (1.0 / acc[HEAD_V:HEAD_V + 1]))
    return outs


def _q_rows(j):
    return slice(j * ATT_TQ, (j + 1) * ATT_TQ)


def _k_rows(c):
    return slice(c * ATT_TK, (c + 1) * ATT_TK)


def _strip_window(strip_ref, hh, j, c):
    qblk = pl.program_id(2) * Q_PER_STEP + j
    off = pl.multiple_of((N_QBLK - 1 - qblk) * ATT_TQ + c * ATT_TK, math.gcd(ATT_TQ, ATT_TK))
    return strip_ref[hh, pl.ds(off, ATT_TK), :]


def _masked_q(q_ref, j, lo, hi):
    q = q_ref[0, _q_rows(j), :]
    return jnp.where(_lanes_between(q.shape, lo, hi), q, jnp.zeros_like(q))


def _diff_kernel(q_ref, k_ref, vt_ref, strip_ref, lam_ref, g_ref, o_ref, *, lambda_init, bounded):
    vt = vt_ref[0]
    lp = lam_ref[...]
    lam = (jnp.exp(jnp.sum(lp[0:1] * lp[1:2], axis=-1, keepdims=True))
           - jnp.exp(jnp.sum(lp[2:3] * lp[3:4], axis=-1, keepdims=True)) + lambda_init)
    subs = 2 * 2
    v_ext = [_values_with_ones(vt, hh) for hh in range(2)]

    def logits(i, c):
        j, sub = divmod(i, subs)
        qm = _masked_q(q_ref, j, DIFF_QK_DIM * sub, DIFF_QK_DIM * (sub + 1))
        return (lax.dot_general(k_ref[0, _k_rows(c), :], qm, _NT, preferred_element_type=F32)
                + _strip_window(strip_ref, sub // 2, j, c))

    parts = _attend(Q_PER_STEP * subs, logits, lambda i: v_ext[(i % subs) // 2], bounded)
    for j in range(Q_PER_STEP):
        outs = []
        for hh in range(2):
            comb = parts[j * subs + 2 * hh] - lam * parts[j * subs + 2 * hh + 1]
            ss = jnp.sum(comb * comb, axis=0, keepdims=True)
            outs.append(comb * lax.rsqrt(ss * (1.0 / DIFF_V_DIM) + EPS))
        y = jnp.concatenate(outs, axis=0).T * g_ref[...] * (1.0 - lambda_init)
        o_ref[0, _q_rows(j), :] = y.astype(BF16)


def _dil_kernel(q_ref, k_ref, vt_ref, strip_ref, o_ref, *, bounded):
    vt = vt_ref[0]
    v_ext = [_values_with_ones(vt, hh) for hh in range(2)]

    def logits(i, c):
        j, hh = divmod(i, 2)
        qm = _masked_q(q_ref, j, DIL_HEAD_DIM * hh, DIL_HEAD_DIM * (hh + 1))
        return (lax.dot_general(k_ref[0, _k_rows(c), :], qm, _NT, preferred_element_type=F32)
                + _strip_window(strip_ref, hh, j, c))

    outs = _attend(Q_PER_STEP * 2, logits, lambda i: v_ext[i % 2], bounded)
    for j in range(Q_PER_STEP):
        o_ref[0, _q_rows(j), :] = jnp.concatenate(outs[2 * j:2 * j + 2], axis=0).T.astype(BF16)


def _mla_kernel(q_ref, k_ref, vt_ref, o_ref, *, bounded):
    vt = vt_ref[0]
    v_ext = [_values_with_ones(vt, hh) for hh in range(2)]

    def logits(i, c):
        j, hh = divmod(i, 2)
        return lax.dot_general(k_ref[0, _k_rows(c), hh * LANE:(hh + 1) * LANE],
                               q_ref[0, _q_rows(j), hh * LANE:(hh + 1) * LANE], _NT,
                               preferred_element_type=F32)

    outs = _attend(Q_PER_STEP * 2, logits, lambda i: v_ext[i % 2], bounded)
    for j in range(Q_PER_STEP):
        o_ref[0, _q_rows(j), :] = jnp.concatenate(outs[2 * j:2 * j + 2], axis=0).T.astype(BF16)


def _attention(kernel, name, logit_bound, q, k, vt, qk_lanes, strips=None, head0=0, extra=()):
    b, s, _ = q.shape
    n_pairs = vt.shape[1] // LANE
    tq = Q_PER_STEP * ATT_TQ
    in_specs = [
        pl.BlockSpec((1, tq, qk_lanes), lambda p, bb, i: (bb, i, p)),
        pl.BlockSpec((1, s, qk_lanes), lambda p, bb, i: (bb, 0, p)),
        pl.BlockSpec((1, LANE, s), lambda p, bb, i: (bb, p, 0)),
    ]
    args = [q, k, vt]
    if strips is not None:
        in_specs.append(pl.BlockSpec((2, STRIP_ROWS, ATT_TQ), lambda p, bb, i: (head0 // 2 + p, 0, 0),
                                     pipeline_mode=pl.Buffered(1)))
        args.append(strips)
    for a in extra:
        per_pair = a.shape[-1] == n_pairs * LANE
        in_specs.append(pl.BlockSpec((1, LANE), lambda p, bb, i: (0, p)) if per_pair
                        else pl.BlockSpec(a.shape, lambda p, bb, i: (0, 0)))
        args.append(a)

    def variant(bounded):
        return pl.pallas_call(
            functools.partial(kernel, bounded=bounded),
            name=name + ("_bounded" if bounded else "_online"),
            grid=(n_pairs, b, s // tq),
            in_specs=in_specs,
            out_specs=pl.BlockSpec((1, tq, LANE), lambda p, bb, i: (bb, i, p)),
            out_shape=jax.ShapeDtypeStruct((b, s, n_pairs * LANE), BF16),
            compiler_params=_params("parallel", "parallel", "parallel"),
        )

    return lax.cond(logit_bound <= LOGIT_BOUND, variant(True), variant(False), *args)


def _logit_bound(gain_q, gain_k, dim, bias_abs=0.0):
    return dim * jnp.max(jnp.abs(gain_q)) * jnp.max(jnp.abs(gain_k)) + bias_abs


def _rope_tables():
    half = MLA_ROPE_DIM // 2
    inv = ROPE_THETA ** (-jnp.arange(half, dtype=F32) / half)
    ang = jnp.arange(SEQ).astype(F32)[:, None] * inv[None, :]
    cos, sin = jnp.cos(ang), jnp.sin(ang)
    ones = jnp.ones((SEQ, MLA_NOPE_DIM), F32)
    tail = LANE - MLA_QK_DIM
    cos_t = jnp.concatenate([ones, cos, cos, jnp.ones((SEQ, tail), F32)], axis=1)
    sin_t = jnp.concatenate([0.0 * ones, -sin, sin, jnp.zeros((SEQ, tail), F32)], axis=1)
    return cos_t, sin_t


def _proj_weights(l, mix_norm, w_in, diff_q_norm, diff_k_norm, dil_q_norm, dil_k_norm,
                  mla_q_norm, mla_q_up, mla_kv_norm, mla_kv_up, mla_qn, mla_kn, cos_t, sin_t):
    wi = w_in[l]
    pad_to_block = LANE - MLA_QK_DIM
    w_ext = jnp.concatenate([wi[:, :SEG_ROPE], jnp.zeros((D_MODEL, MLA_NOPE_DIM), F32),
                             wi[:, SEG_ROPE:], jnp.zeros((D_MODEL, pad_to_block), F32)], axis=1)
    q_up = jnp.pad(mla_q_up[l].reshape(MLA_Q_RANK, MLA_HEADS, MLA_QK_DIM), ((0, 0), (0, 0), (0, pad_to_block)))
    kv = mla_kv_up[l].reshape(MLA_KV_RANK, MLA_HEADS, MLA_NOPE_DIM + MLA_V_DIM)
    kv_k = jnp.pad(kv[:, :, :MLA_NOPE_DIM], ((0, 0), (0, 0), (0, LANE - MLA_NOPE_DIM)))
    kv_v = kv[:, :, MLA_NOPE_DIM:]

    def head_gain(g, heads, scale=1.0, pad=0):
        return jnp.tile(jnp.pad(g.astype(F32) * scale, (0, pad)), heads)[None, :]

    return {
        "g": mix_norm[l][None, :],
        "w_in": w_ext.astype(BF16),
        "q_up": q_up.reshape(MLA_Q_RANK, MLA_PAD).astype(BF16),
        "kv_k": kv_k.reshape(MLA_KV_RANK, MLA_PAD).astype(BF16),
        "kv_v": kv_v.reshape(MLA_KV_RANK, MLA_WIDTH).astype(BF16),
        "gdq": head_gain(diff_q_norm[l], 2 * DIFF_HEADS, DIFF_QK_DIM ** -0.5 * LOG2E),
        "gdk": head_gain(diff_k_norm[l], 2 * DIFF_HEADS),
        "glq": head_gain(dil_q_norm[l], DIL_HEADS, DIL_HEAD_DIM ** -0.5 * LOG2E),
        "glk": head_gain(dil_k_norm[l], DIL_HEADS),
        "gmq": mla_q_norm[l][None, :],
        "gmkv": mla_kv_norm[l][None, :],
        "gqn": head_gain(mla_qn[l], MLA_HEADS, MLA_QK_DIM ** -0.5 * LOG2E, pad_to_block),
        "gkn": head_gain(mla_kn[l], MLA_HEADS, 1.0, pad_to_block),
        "cos": cos_t,
        "sin": sin_t,
    }


def kernel(x, rel_bias, ffn1_norm, ffn1_wg, ffn1_wu, ffn1_wd, mix_norm, w_in, diff_q_norm, diff_k_norm, diff_lambda, diff_subln, dil_q_norm, dil_k_norm, mla_q_norm, mla_q_up, mla_kv_norm, mla_kv_up, mla_qn, mla_kn, w_o, ffn2_norm, ffn2_wg, ffn2_wu, ffn2_wd):
    b, s, d = x.shape
    assert (s, d) == (SEQ, D_MODEL)
    t = b * s
    strips = _bias_strips(_strip_sources(rel_bias))
    cos_t, sin_t = _rope_tables()
    x2d = x.reshape(t, d)
    ffn1 = [_cast_stack(w) for w in (ffn1_wg, ffn1_wu, ffn1_wd)]
    ffn2 = [_cast_stack(w) for w in (ffn2_wg, ffn2_wu, ffn2_wd)]
    w_o_bf16 = _cast_stack(w_o)
    for l in range(DEPTH):
        lambda_init = 0.8 - 0.6 * math.exp(-0.3 * (l + 1))
        x2d = _ffn(x2d, l, ffn1_norm, *ffn1)
        pw = _proj_weights(l, mix_norm, w_in, diff_q_norm, diff_k_norm, dil_q_norm, dil_k_norm,
                           mla_q_norm, mla_q_up, mla_kv_norm, mla_kv_up, mla_qn, mla_kn, cos_t, sin_t)
        dq, dk, dvt, lq, lk, lvt, mq, mk, mvt = _proj(x2d, pw, b)
        tok = lambda a: a.reshape(b, s, -1)
        subln = jnp.tile(diff_subln[l].astype(F32), DIFF_HEADS)[None, :]
        bias_abs = jnp.max(jnp.abs(rel_bias.astype(F32)), axis=0) * LOG2E
        bound_a = _logit_bound(pw["gdq"], pw["gdk"], DIFF_QK_DIM, jnp.max(bias_abs[:DIFF_HEADS]))
        bound_b = _logit_bound(pw["glq"], pw["glk"], DIL_HEAD_DIM,
                               jnp.max(bias_abs[DIFF_HEADS:]) + math.log2(len(DIL_PATTERNS)))
        bound_c = _logit_bound(pw["gqn"], pw["gkn"], MLA_QK_DIM)
        out_a = _attention(functools.partial(_diff_kernel, lambda_init=lambda_init), "diff_attn", bound_a,
                           tok(dq), tok(dk), dvt, LANE, strips, 0, extra=(diff_lambda[l].astype(F32), subln))
        out_b = _attention(_dil_kernel, "dil_attn", bound_b, tok(lq), tok(lk), lvt, LANE, strips, DIFF_HEADS)
        out_c = _attention(_mla_kernel, "mla_attn", bound_c, tok(mq), tok(mk), mvt, 2 * LANE)
        mix = (out_a.reshape(t, -1), out_b.reshape(t, -1), out_c.reshape(t, -1), w_o_bf16)
        x2d = _ffn(x2d, l, ffn2_norm, *ffn2, mix=mix)
    return x2d.reshape(b, s, d)
```

```python
import functools
import math

import jax
import jax.numpy as jnp
import numpy as np
from jax import lax
from jax.experimental import pallas as pl
from jax.experimental.pallas import tpu as pltpu

F32 = jnp.float32
BF16 = jnp.bfloat16

D_MODEL = 1024
SEQ = 2048
DEPTH = 2
EPS = 1e-6
D_FF = 2816
NUM_BUCKETS = 32
MAX_DISTANCE = 1024
ROPE_THETA = 10000.0
NEG_BIG = -1e30
LOG2E = math.log2(math.e)
DIFF_HEADS, DIFF_QK_DIM, DIFF_V_DIM = 4, 32, 64
DIL_HEADS, DIL_HEAD_DIM = 6, 64
DIL_PATTERNS = ((128, 1), (512, 4), (2048, 16))
MLA_HEADS, MLA_Q_RANK, MLA_KV_RANK = 6, 256, 128
MLA_NOPE_DIM, MLA_ROPE_DIM, MLA_V_DIM = 64, 32, 64
MLA_QK_DIM = MLA_NOPE_DIM + MLA_ROPE_DIM
HEAD_V = 64
DIFF_WIDTH = DIFF_HEADS * DIFF_V_DIM
DIL_WIDTH = DIL_HEADS * DIL_HEAD_DIM
MLA_WIDTH = MLA_HEADS * MLA_V_DIM
BIAS_HEADS = DIFF_HEADS + DIL_HEADS

LANE = 128
SUBLANE = 8
BF16_ROWS = 16
VMEM_LIMIT_BYTES = 56 * 1024 * 1024

FFN_TM = 512
CAST_ROWS = 256
PROJ_TM = 512
ATT_TQ = 512
ATT_TK = 512
MAX_SLAB = 128
LOGIT_BOUND = 50.0
STRIP_ROWS = 2 * SEQ - ATT_TQ
STRIP_SRC_W = 2 * SEQ + LANE
N_QBLK = SEQ // ATT_TQ

SEG_DQ, SEG_DK, SEG_DV = 0, 256, 512
SEG_LQ, SEG_LK, SEG_LV = 768, 1152, 1536
SEG_MQ, SEG_MKV, SEG_ROPE = 1920, 2176, 2304
IN_EXT = 2432
MLA_PAD = MLA_HEADS * LANE


def _params(*sem):
    return pltpu.CompilerParams(dimension_semantics=sem, vmem_limit_bytes=VMEM_LIMIT_BYTES)


def _rms(x, g):
    return x * lax.rsqrt(jnp.mean(x * x, axis=-1, keepdims=True) + EPS) * g


def _cast_kernel(w_ref, o_ref):
    o_ref[...] = w_ref[...].astype(BF16)


def _cast_stack(w):
    layers, rows, cols = w.shape
    blk = pl.BlockSpec((1, CAST_ROWS, cols), lambda l, r: (l, r, 0))
    return pl.pallas_call(
        _cast_kernel,
        name="cast_bf16",
        grid=(layers, rows // CAST_ROWS),
        in_specs=[blk],
        out_specs=blk,
        out_shape=jax.ShapeDtypeStruct(w.shape, BF16),
        compiler_params=_params("parallel", "parallel"),
    )(w)


def _ffn_kernel(*refs, fused_out):
    if fused_out:
        x_ref, a_ref, b_ref, c_ref, wo_ref, g_ref, wg_ref, wu_ref, wd_ref, o_ref = refs
        x = x_ref[...]
        row = 0
        for part in (a_ref, b_ref, c_ref):
            width = part.shape[1]
            x += jnp.dot(part[...], wo_ref[row:row + width, :], preferred_element_type=F32)
            row += width
    else:
        x_ref, g_ref, wg_ref, wu_ref, wd_ref, o_ref = refs
        x = x_ref[...]
    n = _rms(x, g_ref[...]).astype(BF16)
    gate = jnp.dot(n, wg_ref[...], preferred_element_type=F32)
    up = jnp.dot(n, wu_ref[...], preferred_element_type=F32)
    h = (gate * jax.nn.sigmoid(gate) * up).astype(BF16)
    o_ref[...] = x + 0.5 * jnp.dot(h, wd_ref[...], preferred_element_type=F32)


def _ffn(x2d, layer, g, wg, wu, wd, mix=None):
    t = x2d.shape[0]
    tm = FFN_TM
    row = lambda i: (i, 0)

    def resident(a):
        return pl.BlockSpec((None,) + a.shape[1:], lambda i: (layer, 0, 0), pipeline_mode=pl.Buffered(1))

    args, in_specs = [x2d], [pl.BlockSpec((tm, D_MODEL), row)]
    if mix is not None:
        a, b, c, wo = mix
        args += [a, b, c, wo]
        in_specs += [pl.BlockSpec((tm, m.shape[1]), row) for m in (a, b, c)] + [resident(wo)]
    weights = [g[:, None, :], wg, wu, wd]
    return pl.pallas_call(
        functools.partial(_ffn_kernel, fused_out=mix is not None),
        name="ffn_out" if mix is not None else "ffn",
        grid=(t // tm,),
        in_specs=in_specs + [resident(w) for w in weights],
        out_specs=pl.BlockSpec((tm, D_MODEL), row),
        out_shape=jax.ShapeDtypeStruct((t, D_MODEL), F32),
        compiler_params=_params("parallel"),
    )(*args, *weights)


def _group_norm(x, gsum, inv_n, gain):
    ss = jnp.dot((x * x).astype(BF16), gsum, preferred_element_type=F32)
    return x * lax.rsqrt(ss * inv_n + EPS) * gain


def _group_norm_chunks(x, gsum, inv_n, gain):
    w = gsum.shape[0]
    parts = [_group_norm(x[:, c:c + w], gsum, inv_n, gain[:, c:c + w]) for c in range(0, x.shape[1], w)]
    return jnp.concatenate(parts, axis=1)


def _rope_blocks(x, cos, sin):
    lane = lax.broadcasted_iota(jnp.int32, (x.shape[0], LANE), 1)
    first_half = lane < MLA_NOPE_DIM + MLA_ROPE_DIM // 2
    parts = []
    for c in range(0, x.shape[1], LANE):
        xb = x[:, c:c + LANE]
        partner = jnp.where(first_half,
                            pltpu.roll(xb, LANE - MLA_ROPE_DIM // 2, 1),
                            pltpu.roll(xb, MLA_ROPE_DIM // 2, 1))
        parts.append(xb * cos + partner * sin)
    return jnp.concatenate(parts, axis=1)


def _proj_kernel(x_ref, g_ref, win_ref, qup_ref, kvk_ref, kvv_ref,
                 gdq_ref, gdk_ref, glq_ref, glk_ref, gmq_ref, gmkv_ref, gqn_ref, gkn_ref,
                 cos_ref, sin_ref, g32_ref, g64_ref, g128_ref,
                 dq_ref, dk_ref, dvt_ref, lq_ref, lk_ref, lvt_ref, mq_ref, mk_ref, mvt_ref):
    h = _rms(x_ref[...], g_ref[...]).astype(BF16)
    proj = jnp.dot(h, win_ref[...], preferred_element_type=F32)

    def seg(lo, hi):
        return proj[:, lo:hi]

    g32, g64, g128 = g32_ref[...], g64_ref[...], g128_ref[...]
    dq_ref[...] = _group_norm(seg(SEG_DQ, SEG_DK), g32, 1.0 / DIFF_QK_DIM, gdq_ref[...]).astype(BF16)
    dk_ref[...] = _group_norm(seg(SEG_DK, SEG_DV), g32, 1.0 / DIFF_QK_DIM, gdk_ref[...]).astype(BF16)
    dvt_ref[0] = seg(SEG_DV, SEG_LQ).T.astype(BF16)
    lqk = _group_norm_chunks(seg(SEG_LQ, SEG_LV), g64, 1.0 / DIL_HEAD_DIM,
                             jnp.concatenate([glq_ref[...], glk_ref[...]], axis=1))
    lq_ref[...] = lqk[:, :DIL_WIDTH].astype(BF16)
    lk_ref[...] = lqk[:, DIL_WIDTH:].astype(BF16)
    lvt_ref[0] = seg(SEG_LV, SEG_MQ).T.astype(BF16)

    cos, sin = cos_ref[...], sin_ref[...]
    q_lat = _rms(seg(SEG_MQ, SEG_MKV), gmq_ref[...]).astype(BF16)
    q = jnp.dot(q_lat, qup_ref[...], preferred_element_type=F32)
    q = _group_norm_chunks(q, g128, 1.0 / MLA_QK_DIM, gqn_ref[...])
    mq_ref[...] = _rope_blocks(q, cos, sin).astype(BF16)

    c_kv = _rms(seg(SEG_MKV, SEG_ROPE), gmkv_ref[...]).astype(BF16)
    k_rope = seg(SEG_ROPE, IN_EXT)
    k = jnp.dot(c_kv, kvk_ref[...], preferred_element_type=F32)
    k = k + jnp.concatenate([k_rope] * MLA_HEADS, axis=1)
    k = _group_norm_chunks(k, g128, 1.0 / MLA_QK_DIM, gkn_ref[...])
    mk_ref[...] = _rope_blocks(k, cos, sin).astype(BF16)
    mvt_ref[0] = jnp.dot(c_kv, kvv_ref[...], preferred_element_type=F32).T.astype(BF16)


def _block_diag_ones(width, group):
    idx = np.arange(width) // group
    return jnp.asarray((idx[:, None] == idx[None, :]).astype(np.float32), dtype=BF16)


def _proj(x2d, pw, batch):
    t = x2d.shape[0]
    tm = PROJ_TM
    row = lambda i: (i, 0)
    fixed = lambda i: (0, 0)
    pos_blocks = SEQ // tm
    col = lambda i: (i // pos_blocks, 0, i % pos_blocks)

    def full(a):
        return pl.BlockSpec(a.shape, fixed)

    consts = [pw["g"], pw["w_in"], pw["q_up"], pw["kv_k"], pw["kv_v"],
              pw["gdq"], pw["gdk"], pw["glq"], pw["glk"], pw["gmq"], pw["gmkv"], pw["gqn"], pw["gkn"]]
    tables = [pw["cos"], pw["sin"]]
    gsums = [_block_diag_ones(256, DIFF_QK_DIM), _block_diag_ones(256, DIL_HEAD_DIM),
             _block_diag_ones(256, LANE)]
    outs = [(256, False), (256, False), (DIFF_WIDTH, True), (DIL_WIDTH, False), (DIL_WIDTH, False),
            (DIL_WIDTH, True), (MLA_PAD, False), (MLA_PAD, False), (MLA_WIDTH, True)]
    return pl.pallas_call(
        _proj_kernel,
        name="proj",
        grid=(t // tm,),
        in_specs=([pl.BlockSpec((tm, D_MODEL), row)] + [full(a) for a in consts]
                  + [pl.BlockSpec((tm, LANE), lambda i: (i % pos_blocks, 0)) for _ in tables]
                  + [full(a) for a in gsums]),
        out_specs=[pl.BlockSpec((1, w, tm), col) if tr else pl.BlockSpec((tm, w), row) for w, tr in outs],
        out_shape=[jax.ShapeDtypeStruct((batch, w, SEQ) if tr else (t, w), BF16) for w, tr in outs],
        compiler_params=_params("parallel"),
    )(x2d, *consts, *tables, *gsums)


def _strip_kernel(src_ref, o_ref):
    chunk_w = ATT_TQ + LANE
    for a in range(STRIP_ROWS // SUBLANE):
        start = STRIP_ROWS - SUBLANE - SUBLANE * a
        lo = (start // LANE) * LANE
        chunk = src_ref[0, :, lo:lo + chunk_w]
        shift = start - lo
        if shift:
            chunk = pltpu.roll(chunk, chunk_w - shift, 1)
        o_ref[0, a * SUBLANE:(a + 1) * SUBLANE, :] = chunk[:, :ATT_TQ]


def _bias_strips(src):
    nh = src.shape[0]
    return pl.pallas_call(
        _strip_kernel,
        name="bias_strips",
        grid=(nh,),
        in_specs=[pl.BlockSpec((1, SUBLANE, STRIP_SRC_W), lambda h: (h, 0, 0))],
        out_specs=pl.BlockSpec((1, STRIP_ROWS, ATT_TQ), lambda h: (h, 0, 0)),
        out_shape=jax.ShapeDtypeStruct((nh, STRIP_ROWS, ATT_TQ), F32),
        compiler_params=_params("parallel"),
    )(src)


def _t5_bucket(rel):
    half = NUM_BUCKETS // 2
    max_exact = half // 2
    n = jnp.abs(rel)
    nf = jnp.maximum(n, 1).astype(F32)
    large = max_exact + (jnp.log(nf / max_exact) / math.log(MAX_DISTANCE / max_exact)
                         * (half - max_exact)).astype(jnp.int32)
    large = jnp.minimum(large, half - 1)
    return jnp.where(rel > 0, half, 0) + jnp.where(n < max_exact, n, large)


def _strip_sources(rel_bias):
    rel = np.arange(-(SEQ - 1), SEQ)
    mult = np.zeros(rel.shape, np.int64)
    for window, dil in DIL_PATTERNS:
        mult += (rel % dil == 0) & (np.abs(rel) // dil <= window // (2 * dil))
    table = rel_bias.astype(F32)[_t5_bucket(jnp.asarray(rel, jnp.int32))]
    logm = jnp.asarray(np.log(np.maximum(mult, 1)), F32)[:, None]
    dil_vec = jnp.where(jnp.asarray(mult > 0)[:, None], table[:, DIFF_HEADS:] + logm, NEG_BIG)
    vec = jnp.concatenate([table[:, :DIFF_HEADS], dil_vec], axis=1).T * LOG2E
    rev = jnp.pad(vec[:, ::-1], ((0, 0), (0, STRIP_SRC_W + SUBLANE - vec.shape[1])))
    return jnp.stack([rev[:, SUBLANE - 1 - s:SUBLANE - 1 - s + STRIP_SRC_W] for s in range(SUBLANE)], axis=1)


_NT = (((1,), (1,)), ((), ()))


def _lanes_between(shape, lo, hi):
    lane = lax.broadcasted_iota(jnp.int32, shape, 1)
    return (lane >= lo) & (lane < hi)


def _values_with_ones(vt, hh):
    return jnp.concatenate([vt[hh * HEAD_V:(hh + 1) * HEAD_V], jnp.ones((BF16_ROWS, vt.shape[1]), BF16)], axis=0)


def _col_max(s_t):
    part = jnp.max(s_t.reshape(s_t.shape[0] // MAX_SLAB, MAX_SLAB, s_t.shape[1]), axis=0)
    return jnp.max(part, axis=0, keepdims=True)


def _attend(n_items, logits_fn, values_fn, bounded, needed=lambda item, chunk: True):
    stages = [(i, c) for i in range(n_items) for c in range(SEQ // ATT_TK) if needed(i, c)]
    outs = []
    s_next = logits_fn(*stages[0])
    m = acc = None
    for idx, (item, c) in enumerate(stages):
        first = idx == 0 or stages[idx - 1][0] != item
        last = idx + 1 == len(stages) or stages[idx + 1][0] != item
        s_cur = s_next
        if idx + 1 < len(stages):
            s_next = logits_fn(*stages[idx + 1])
        values = values_fn(item)[:, c * ATT_TK:(c + 1) * ATT_TK]
        if bounded:
            pv = jnp.dot(values, jnp.exp2(s_cur).astype(BF16), preferred_element_type=F32)
            acc = pv if first else acc + pv
        else:
            c_max = _col_max(s_cur)
            m_new = c_max if first else jnp.maximum(m, c_max)
            pv = jnp.dot(values, jnp.exp2((s_cur - m_new).astype(BF16)), preferred_element_type=F32)
            acc = pv if first else acc * jnp.exp2(m - m_new) + pv
            m = m_new
        if last:
            outs.append(acc[:HEAD_V] * (1.0 / acc[HEAD_V:HEAD_V + 1]))
    return outs


def _q_rows(j):
    return slice(j * ATT_TQ, (j + 1) * ATT_TQ)


def _k_rows(c):
    return slice(c * ATT_TK, (c + 1) * ATT_TK)


def _strip_window(strip_ref, hh, qblk, c):
    off = (N_QBLK - 1 - qblk) * ATT_TQ + c * ATT_TK
    return strip_ref[hh, off:off + ATT_TK, :]


def _dil_chunk_needed(qblk, c):
    reach = max(window // 2 for window, _ in DIL_PATTERNS)
    q_lo, q_hi = qblk * ATT_TQ, (qblk + 1) * ATT_TQ - 1
    k_lo, k_hi = c * ATT_TK, (c + 1) * ATT_TK - 1
    return max(k_lo - q_hi, q_lo - k_hi, 0) <= reach


def _masked_q(q_ref, j, lo, hi):
    q = q_ref[0, _q_rows(j), :]
    return jnp.where(_lanes_between(q.shape, lo, hi), q, jnp.zeros_like(q))


def _diff_kernel(q_ref, k_ref, vt_ref, strip_ref, lam_ref, g_ref, o_ref, *, lambda_init, bounded):
    vt = vt_ref[0]
    lp = lam_ref[...]
    lam = (jnp.exp(jnp.sum(lp[0:1] * lp[1:2], axis=-1, keepdims=True))
           - jnp.exp(jnp.sum(lp[2:3] * lp[3:4], axis=-1, keepdims=True)) + lambda_init)
    subs = 2 * 2
    v_ext = [_values_with_ones(vt, hh) for hh in range(2)]

    def logits(i, c):
        j, sub = divmod(i, subs)
        qm = _masked_q(q_ref, j, DIFF_QK_DIM * sub, DIFF_QK_DIM * (sub + 1))
        return (lax.dot_general(k_ref[0, _k_rows(c), :], qm, _NT, preferred_element_type=F32)
                + _strip_window(strip_ref, sub // 2, j, c))

    parts = _attend(N_QBLK * subs, logits, lambda i: v_ext[(i % subs) // 2], bounded)
    for j in range(N_QBLK):
        outs = []
        for hh in range(2):
            comb = parts[j * subs + 2 * hh] - lam * parts[j * subs + 2 * hh + 1]
            ss = jnp.sum(comb * comb, axis=0, keepdims=True)
            outs.append(comb * lax.rsqrt(ss * (1.0 / DIFF_V_DIM) + EPS))
        y = jnp.concatenate(outs, axis=0).T * g_ref[...] * (1.0 - lambda_init)
        o_ref[0, _q_rows(j), :] = y.astype(BF16)


def _dil_kernel(q_ref, k_ref, vt_ref, strip_ref, o_ref, *, bounded):
    vt = vt_ref[0]
    v_ext = [_values_with_ones(vt, hh) for hh in range(2)]

    def logits(i, c):
        j, hh = divmod(i, 2)
        qm = _masked_q(q_ref, j, DIL_HEAD_DIM * hh, DIL_HEAD_DIM * (hh + 1))
        return (lax.dot_general(k_ref[0, _k_rows(c), :], qm, _NT, preferred_element_type=F32)
                + _strip_window(strip_ref, hh, j, c))

    outs = _attend(N_QBLK * 2, logits, lambda i: v_ext[i % 2], bounded,
                   needed=lambda i, c: _dil_chunk_needed(i // 2, c))
    for j in range(N_QBLK):
        o_ref[0, _q_rows(j), :] = jnp.concatenate(outs[2 * j:2 * j + 2], axis=0).T.astype(BF16)


def _mla_kernel(q_ref, k_ref, vt_ref, o_ref, *, bounded):
    vt = vt_ref[0]
    v_ext = [_values_with_ones(vt, hh) for hh in range(2)]

    def logits(i, c):
        j, hh = divmod(i, 2)
        return lax.dot_general(k_ref[0, _k_rows(c), hh * LANE:(hh + 1) * LANE],
                               q_ref[0, _q_rows(j), hh * LANE:(hh + 1) * LANE], _NT,
                               preferred_element_type=F32)

    outs = _attend(N_QBLK * 2, logits, lambda i: v_ext[i % 2], bounded)
    for j in range(N_QBLK):
        o_ref[0, _q_rows(j), :] = jnp.concatenate(outs[2 * j:2 * j + 2], axis=0).T.astype(BF16)


def _attention(kernel, name, logit_bound, q, k, vt, qk_lanes, strips=None, head0=0, extra=()):
    b, s, _ = q.shape
    n_pairs = vt.shape[1] // LANE
    in_specs = [
        pl.BlockSpec((1, s, qk_lanes), lambda p, bb: (bb, 0, p)),
        pl.BlockSpec((1, s, qk_lanes), lambda p, bb: (bb, 0, p)),
        pl.BlockSpec((1, LANE, s), lambda p, bb: (bb, p, 0)),
    ]
    args = [q, k, vt]
    if strips is not None:
        in_specs.append(pl.BlockSpec((2, STRIP_ROWS, ATT_TQ), lambda p, bb: (head0 // 2 + p, 0, 0),
                                     pipeline_mode=pl.Buffered(1)))
        args.append(strips)
    for a in extra:
        per_pair = a.shape[-1] == n_pairs * LANE
        in_specs.append(pl.BlockSpec((1, LANE), lambda p, bb: (0, p)) if per_pair
                        else pl.BlockSpec(a.shape, lambda p, bb: (0, 0)))
        args.append(a)

    def variant(bounded):
        return pl.pallas_call(
            functools.partial(kernel, bounded=bounded),
            name=name + ("_bounded" if bounded else "_online"),
            grid=(n_pairs, b),
            in_specs=in_specs,
            out_specs=pl.BlockSpec((1, s, LANE), lambda p, bb: (bb, 0, p)),
            out_shape=jax.ShapeDtypeStruct((b, s, n_pairs * LANE), BF16),
            compiler_params=_params("parallel", "parallel"),
        )

    return lax.cond(logit_bound <= LOGIT_BOUND, variant(True), variant(False), *args)


def _logit_bound(gain_q, gain_k, dim, bias_abs=0.0):
    return dim * jnp.max(jnp.abs(gain_q)) * jnp.max(jnp.abs(gain_k)) + bias_abs


def _rope_tables():
    half = MLA_ROPE_DIM // 2
    inv = ROPE_THETA ** (-jnp.arange(half, dtype=F32) / half)
    ang = jnp.arange(SEQ).astype(F32)[:, None] * inv[None, :]
    cos, sin = jnp.cos(ang), jnp.sin(ang)
    ones = jnp.ones((SEQ, MLA_NOPE_DIM), F32)
    tail = LANE - MLA_QK_DIM
    cos_t = jnp.concatenate([ones, cos, cos, jnp.ones((SEQ, tail), F32)], axis=1)
    sin_t = jnp.concatenate([0.0 * ones, -sin, sin, jnp.zeros((SEQ, tail), F32)], axis=1)
    return cos_t, sin_t


def _proj_weights(l, mix_norm, w_in, diff_q_norm, diff_k_norm, dil_q_norm, dil_k_norm,
                  mla_q_norm, mla_q_up, mla_kv_norm, mla_kv_up, mla_qn, mla_kn, cos_t, sin_t):
    wi = w_in[l]
    pad_to_block = LANE - MLA_QK_DIM
    w_ext = jnp.concatenate([wi[:, :SEG_ROPE], jnp.zeros((D_MODEL, MLA_NOPE_DIM), F32),
                             wi[:, SEG_ROPE:], jnp.zeros((D_MODEL, pad_to_block), F32)], axis=1)
    q_up = jnp.pad(mla_q_up[l].reshape(MLA_Q_RANK, MLA_HEADS, MLA_QK_DIM), ((0, 0), (0, 0), (0, pad_to_block)))
    kv = mla_kv_up[l].reshape(MLA_KV_RANK, MLA_HEADS, MLA_NOPE_DIM + MLA_V_DIM)
    kv_k = jnp.pad(kv[:, :, :MLA_NOPE_DIM], ((0, 0), (0, 0), (0, LANE - MLA_NOPE_DIM)))
    kv_v = kv[:, :, MLA_NOPE_DIM:]

    def head_gain(g, heads, scale=1.0, pad=0):
        return jnp.tile(jnp.pad(g.astype(F32) * scale, (0, pad)), heads)[None, :]

    return {
        "g": mix_norm[l][None, :],
        "w_in": w_ext.astype(BF16),
        "q_up": q_up.reshape(MLA_Q_RANK, MLA_PAD).astype(BF16),
        "kv_k": kv_k.reshape(MLA_KV_RANK, MLA_PAD).astype(BF16),
        "kv_v": kv_v.reshape(MLA_KV_RANK, MLA_WIDTH).astype(BF16),
        "gdq": head_gain(diff_q_norm[l], 2 * DIFF_HEADS, DIFF_QK_DIM ** -0.5 * LOG2E),
        "gdk": head_gain(diff_k_norm[l], 2 * DIFF_HEADS),
        "glq": head_gain(dil_q_norm[l], DIL_HEADS, DIL_HEAD_DIM ** -0.5 * LOG2E),
        "glk": head_gain(dil_k_norm[l], DIL_HEADS),
        "gmq": mla_q_norm[l][None, :],
        "gmkv": mla_kv_norm[l][None, :],
        "gqn": head_gain(mla_qn[l], MLA_HEADS, MLA_QK_DIM ** -0.5 * LOG2E, pad_to_block),
        "gkn": head_gain(mla_kn[l], MLA_HEADS, 1.0, pad_to_block),
        "cos": cos_t,
        "sin": sin_t,
    }


def kernel(x, rel_bias, ffn1_norm, ffn1_wg, ffn1_wu, ffn1_wd, mix_norm, w_in, diff_q_norm, diff_k_norm, diff_lambda, diff_subln, dil_q_norm, dil_k_norm, mla_q_norm, mla_q_up, mla_kv_norm, mla_kv_up, mla_qn, mla_kn, w_o, ffn2_norm, ffn2_wg, ffn2_wu, ffn2_wd):
    b, s, d = x.shape
    assert (s, d) == (SEQ, D_MODEL)
    t = b * s
    strips = _bias_strips(_strip_sources(rel_bias))
    cos_t, sin_t = _rope_tables()
    x2d = x.reshape(t, d)
    ffn1 = [_cast_stack(w) for w in (ffn1_wg, ffn1_wu, ffn1_wd)]
    ffn2 = [_cast_stack(w) for w in (ffn2_wg, ffn2_wu, ffn2_wd)]
    w_o_bf16 = _cast_stack(w_o)
    for l in range(DEPTH):
        lambda_init = 0.8 - 0.6 * math.exp(-0.3 * (l + 1))
        x2d = _ffn(x2d, l, ffn1_norm, *ffn1)
        pw = _proj_weights(l, mix_norm, w_in, diff_q_norm, diff_k_norm, dil_q_norm, dil_k_norm,
                           mla_q_norm, mla_q_up, mla_kv_norm, mla_kv_up, mla_qn, mla_kn, cos_t, sin_t)
        dq, dk, dvt, lq, lk, lvt, mq, mk, mvt = _proj(x2d, pw, b)
        tok = lambda a: a.reshape(b, s, -1)
        subln = jnp.tile(diff_subln[l].astype(F32), DIFF_HEADS)[None, :]
        bias_abs = jnp.max(jnp.abs(rel_bias.astype(F32)), axis=0) * LOG2E
        bound_a = _logit_bound(pw["gdq"], pw["gdk"], DIFF_QK_DIM, jnp.max(bias_abs[:DIFF_HEADS]))
        bound_b = _logit_bound(pw["glq"], pw["glk"], DIL_HEAD_DIM,
                               jnp.max(bias_abs[DIFF_HEADS:]) + math.log2(len(DIL_PATTERNS)))
        bound_c = _logit_bound(pw["gqn"], pw["gkn"], MLA_QK_DIM)
        out_a = _attention(functools.partial(_diff_kernel, lambda_init=lambda_init), "diff_attn", bound_a,
                           tok(dq), tok(dk), dvt, LANE, strips, 0, extra=(diff_lambda[l].astype(F32), subln))
        out_b = _attention(_dil_kernel, "dil_attn", bound_b, tok(lq), tok(lk), lvt, LANE, strips, DIFF_HEADS)
        out_c = _attention(_mla_kernel, "mla_attn", bound_c, tok(mq), tok(mk), mvt, 2 * LANE)
        mix = (out_a.reshape(t, -1), out_b.reshape(t, -1), out_c.reshape(t, -1), w_o_bf16)
        x2d = _ffn(x2d, l, ffn2_norm, *ffn2, mix=mix)
    return x2d.reshape(b, s, d)
```

```python
import functools
import math

import jax
import jax.numpy as jnp
import numpy as np
from jax import lax
from jax.experimental import pallas as pl
from jax.experimental.pallas import tpu as pltpu

F32 = jnp.float32
BF16 = jnp.bfloat16

D_MODEL = 1024
SEQ = 2048
DEPTH = 2
EPS = 1e-6
D_FF = 2816
NUM_BUCKETS = 32
MAX_DISTANCE = 1024
ROPE_THETA = 10000.0
NEG_BIG = -1e30
LOG2E = math.log2(math.e)
DIFF_HEADS, DIFF_QK_DIM, DIFF_V_DIM = 4, 32, 64
DIL_HEADS, DIL_HEAD_DIM = 6, 64
DIL_PATTERNS = ((128, 1), (512, 4), (2048, 16))
MLA_HEADS, MLA_Q_RANK, MLA_KV_RANK = 6, 256, 128
MLA_NOPE_DIM, MLA_ROPE_DIM, MLA_V_DIM = 64, 32, 64
MLA_QK_DIM = MLA_NOPE_DIM + MLA_ROPE_DIM
HEAD_V = 64
DIFF_WIDTH = DIFF_HEADS * DIFF_V_DIM
DIL_WIDTH = DIL_HEADS * DIL_HEAD_DIM
MLA_WIDTH = MLA_HEADS * MLA_V_DIM
BIAS_HEADS = DIFF_HEADS + DIL_HEADS

LANE = 128
SUBLANE = 8
BF16_ROWS = 16
VMEM_LIMIT_BYTES = 56 * 1024 * 1024

FFN_TM = 512
CAST_ROWS = 256
PROJ_TM = 512
PROJ_SPLIT = 2
ATT_TQ = 512
ATT_TK = 512
MAX_SLAB = 128
LOGIT_BOUND = 50.0
STRIP_ROWS = 2 * SEQ - ATT_TQ
STRIP_SRC_W = 2 * SEQ + LANE
N_QBLK = SEQ // ATT_TQ

SEG_DQ, SEG_DK, SEG_DV = 0, 256, 512
SEG_LQ, SEG_LK, SEG_LV = 768, 1152, 1536
SEG_MQ, SEG_MKV, SEG_ROPE = 1920, 2176, 2304
IN_EXT = 2432
MLA_PAD = MLA_HEADS * LANE


def _params(*sem):
    return pltpu.CompilerParams(dimension_semantics=sem, vmem_limit_bytes=VMEM_LIMIT_BYTES)


def _rms(x, g):
    return x * lax.rsqrt(jnp.mean(x * x, axis=-1, keepdims=True) + EPS) * g


def _cast_kernel(w_ref, o_ref):
    o_ref[...] = w_ref[...].astype(BF16)


def _cast_stack(w):
    layers, rows, cols = w.shape
    blk = pl.BlockSpec((1, CAST_ROWS, cols), lambda l, r: (l, r, 0))
    return pl.pallas_call(
        _cast_kernel,
        name="cast_bf16",
        grid=(layers, rows // CAST_ROWS),
        in_specs=[blk],
        out_specs=blk,
        out_shape=jax.ShapeDtypeStruct(w.shape, BF16),
        compiler_params=_params("parallel", "parallel"),
    )(w)


def _ffn_kernel(*refs, fused_out):
    if fused_out:
        x_ref, a_ref, b_ref, c_ref, wo_ref, g_ref, wg_ref, wu_ref, wd_ref, o_ref = refs
        mix_refs = (a_ref, b_ref, c_ref)
    else:
        x_ref, g_ref, wg_ref, wu_ref, wd_ref, o_ref = refs
        mix_refs = ()
    x = x_ref[...]
    row = 0
    for part in mix_refs:
        width = part.shape[1]
        x += jnp.dot(part[...], wo_ref[row:row + width, :], preferred_element_type=F32)
        row += width
    n = _rms(x, g_ref[...]).astype(BF16)
    gate = jnp.dot(n, wg_ref[...], preferred_element_type=F32)
    up = jnp.dot(n, wu_ref[...], preferred_element_type=F32)
    h = (gate * jax.nn.sigmoid(gate) * up).astype(BF16)
    o_ref[...] = x + 0.5 * jnp.dot(h, wd_ref[...], preferred_element_type=F32)


def _ffn(x2d, layer, g, wg, wu, wd, mix=None):
    t = x2d.shape[0]
    tm = FFN_TM
    row = lambda i: (i, 0)

    def resident(a):
        return pl.BlockSpec((None,) + a.shape[1:], lambda i: (layer, 0, 0), pipeline_mode=pl.Buffered(1))

    args, in_specs = [x2d], [pl.BlockSpec((tm, D_MODEL), row)]
    if mix is not None:
        a, b, c, wo = mix
        args += [a, b, c, wo]
        in_specs += [pl.BlockSpec((tm, m.shape[1]), row) for m in (a, b, c)] + [resident(wo)]
    weights = [g[:, None, :], wg, wu, wd]
    return pl.pallas_call(
        functools.partial(_ffn_kernel, fused_out=mix is not None),
        name="ffn_out" if mix is not None else "ffn",
        grid=(t // tm,),
        in_specs=in_specs + [resident(w) for w in weights],
        out_specs=pl.BlockSpec((tm, D_MODEL), row),
        out_shape=jax.ShapeDtypeStruct((t, D_MODEL), F32),
        compiler_params=_params("parallel"),
    )(*args, *weights)


def _group_norm(x, gsum, inv_n, gain):
    ss = jnp.dot((x * x).astype(BF16), gsum, preferred_element_type=F32)
    return x * lax.rsqrt(ss * inv_n + EPS) * gain


def _group_norm_chunks(x, gsum, inv_n, gain):
    w = gsum.shape[0]
    parts = [_group_norm(x[:, c:c + w], gsum, inv_n, gain[:, c:c + w]) for c in range(0, x.shape[1], w)]
    return jnp.concatenate(parts, axis=1)


def _rope_blocks(x, cos, sin):
    lane = lax.broadcasted_iota(jnp.int32, (x.shape[0], LANE), 1)
    first_half = lane < MLA_NOPE_DIM + MLA_ROPE_DIM // 2
    parts = []
    for c in range(0, x.shape[1], LANE):
        xb = x[:, c:c + LANE]
        partner = jnp.where(first_half,
                            pltpu.roll(xb, LANE - MLA_ROPE_DIM // 2, 1),
                            pltpu.roll(xb, MLA_ROPE_DIM // 2, 1))
        parts.append(xb * cos + partner * sin)
    return jnp.concatenate(parts, axis=1)


def _proj_kernel(x_ref, g_ref, win_ref, qup_ref, kvk_ref, kvv_ref,
                 gdq_ref, gdk_ref, glq_ref, glk_ref, gmq_ref, gmkv_ref, gqn_ref, gkn_ref,
                 cos_ref, sin_ref, g32_ref, g64_ref, g128_ref,
                 dq_ref, dk_ref, dvt_ref, lq_ref, lk_ref, lvt_ref, mq_ref, mk_ref, mvt_ref):
    g32, g64, g128 = g32_ref[...], g64_ref[...], g128_ref[...]
    sub = PROJ_TM // PROJ_SPLIT
    for r in range(PROJ_SPLIT):
        rows = slice(r * sub, (r + 1) * sub)
        h = _rms(x_ref[rows, :], g_ref[...]).astype(BF16)
        proj = jnp.dot(h, win_ref[...], preferred_element_type=F32)

        def seg(lo, hi):
            return proj[:, lo:hi]

        dq_ref[rows, :] = _group_norm(seg(SEG_DQ, SEG_DK), g32, 1.0 / DIFF_QK_DIM, gdq_ref[...]).astype(BF16)
        dk_ref[rows, :] = _group_norm(seg(SEG_DK, SEG_DV), g32, 1.0 / DIFF_QK_DIM, gdk_ref[...]).astype(BF16)
        dvt_ref[0, :, rows] = seg(SEG_DV, SEG_LQ).T.astype(BF16)
        lqk = _group_norm_chunks(seg(SEG_LQ, SEG_LV), g64, 1.0 / DIL_HEAD_DIM,
                                 jnp.concatenate([glq_ref[...], glk_ref[...]], axis=1))
        lq_ref[rows, :] = lqk[:, :DIL_WIDTH].astype(BF16)
        lk_ref[rows, :] = lqk[:, DIL_WIDTH:].astype(BF16)
        lvt_ref[0, :, rows] = seg(SEG_LV, SEG_MQ).T.astype(BF16)

        cos, sin = cos_ref[rows, :], sin_ref[rows, :]
        q_lat = _rms(seg(SEG_MQ, SEG_MKV), gmq_ref[...]).astype(BF16)
        q = jnp.dot(q_lat, qup_ref[...], preferred_element_type=F32)
        q = _group_norm_chunks(q, g128, 1.0 / MLA_QK_DIM, gqn_ref[...])
        mq_ref[rows, :] = _rope_blocks(q, cos, sin).astype(BF16)

        c_kv = _rms(seg(SEG_MKV, SEG_ROPE), gmkv_ref[...]).astype(BF16)
        k_rope = seg(SEG_ROPE, IN_EXT)
        k = jnp.dot(c_kv, kvk_ref[...], preferred_element_type=F32)
        k = k + jnp.concatenate([k_rope] * MLA_HEADS, axis=1)
        k = _group_norm_chunks(k, g128, 1.0 / MLA_QK_DIM, gkn_ref[...])
        mk_ref[rows, :] = _rope_blocks(k, cos, sin).astype(BF16)
        mvt_ref[0, :, rows] = jnp.dot(c_kv, kvv_ref[...], preferred_element_type=F32).T.astype(BF16)


def _block_diag_ones(width, group):
    idx = np.arange(width) // group
    return jnp.asarray((idx[:, None] == idx[None, :]).astype(np.float32), dtype=BF16)


def _proj(x2d, pw, batch):
    t = x2d.shape[0]
    tm = PROJ_TM
    row = lambda i: (i, 0)
    fixed = lambda i: (0, 0)
    pos_blocks = SEQ // tm
    col = lambda i: (i // pos_blocks, 0, i % pos_blocks)

    def full(a):
        return pl.BlockSpec(a.shape, fixed)

    consts = [pw["g"], pw["w_in"], pw["q_up"], pw["kv_k"], pw["kv_v"],
              pw["gdq"], pw["gdk"], pw["glq"], pw["glk"], pw["gmq"], pw["gmkv"], pw["gqn"], pw["gkn"]]
    tables = [pw["cos"], pw["sin"]]
    gsums = [_block_diag_ones(256, DIFF_QK_DIM), _block_diag_ones(256, DIL_HEAD_DIM),
             _block_diag_ones(256, LANE)]
    outs = [(256, False), (256, False), (DIFF_WIDTH, True), (DIL_WIDTH, False), (DIL_WIDTH, False),
            (DIL_WIDTH, True), (MLA_PAD, False), (MLA_PAD, False), (MLA_WIDTH, True)]
    return pl.pallas_call(
        _proj_kernel,
        name="proj",
        grid=(t // tm,),
        in_specs=([pl.BlockSpec((tm, D_MODEL), row)] + [full(a) for a in consts]
                  + [pl.BlockSpec((tm, LANE), lambda i: (i % pos_blocks, 0)) for _ in tables]
                  + [full(a) for a in gsums]),
        out_specs=[pl.BlockSpec((1, w, tm), col) if tr else pl.BlockSpec((tm, w), row) for w, tr in outs],
        out_shape=[jax.ShapeDtypeStruct((batch, w, SEQ) if tr else (t, w), BF16) for w, tr in outs],
        compiler_params=_params("parallel"),
    )(x2d, *consts, *tables, *gsums)


def _strip_kernel(src_ref, o_ref):
    chunk_w = ATT_TQ + LANE
    for a in range(STRIP_ROWS // SUBLANE):
        start = STRIP_ROWS - SUBLANE - SUBLANE * a
        lo = (start // LANE) * LANE
        chunk = src_ref[0, :, lo:lo + chunk_w]
        shift = start - lo
        if shift:
            chunk = pltpu.roll(chunk, chunk_w - shift, 1)
        o_ref[0, a * SUBLANE:(a + 1) * SUBLANE, :] = chunk[:, :ATT_TQ]


def _bias_strips(src):
    nh = src.shape[0]
    return pl.pallas_call(
        _strip_kernel,
        name="bias_strips",
        grid=(nh,),
        in_specs=[pl.BlockSpec((1, SUBLANE, STRIP_SRC_W), lambda h: (h, 0, 0))],
        out_specs=pl.BlockSpec((1, STRIP_ROWS, ATT_TQ), lambda h: (h, 0, 0)),
        out_shape=jax.ShapeDtypeStruct((nh, STRIP_ROWS, ATT_TQ), F32),
        compiler_params=_params("parallel"),
    )(src)


def _t5_bucket(rel):
    half = NUM_BUCKETS // 2
    max_exact = half // 2
    n = jnp.abs(rel)
    nf = jnp.maximum(n, 1).astype(F32)
    large = max_exact + (jnp.log(nf / max_exact) / math.log(MAX_DISTANCE / max_exact)
                         * (half - max_exact)).astype(jnp.int32)
    large = jnp.minimum(large, half - 1)
    return jnp.where(rel > 0, half, 0) + jnp.where(n < max_exact, n, large)


def _strip_sources(rel_bias):
    rel = np.arange(-(SEQ - 1), SEQ)
    mult = np.zeros(rel.shape, np.int64)
    for window, dil in DIL_PATTERNS:
        mult += (rel % dil == 0) & (np.abs(rel) // dil <= window // (2 * dil))
    table = rel_bias.astype(F32)[_t5_bucket(jnp.asarray(rel, jnp.int32))]
    logm = jnp.asarray(np.log(np.maximum(mult, 1)), F32)[:, None]
    dil_vec = jnp.where(jnp.asarray(mult > 0)[:, None], table[:, DIFF_HEADS:] + logm, NEG_BIG)
    vec = jnp.concatenate([table[:, :DIFF_HEADS], dil_vec], axis=1).T * LOG2E
    rev = jnp.pad(vec[:, ::-1], ((0, 0), (0, STRIP_SRC_W + SUBLANE - vec.shape[1])))
    return jnp.stack([rev[:, SUBLANE - 1 - s:SUBLANE - 1 - s + STRIP_SRC_W] for s in range(SUBLANE)], axis=1)


_NT = (((1,), (1,)), ((), ()))


def _lanes_between(shape, lo, hi):
    lane = lax.broadcasted_iota(jnp.int32, shape, 1)
    return (lane >= lo) & (lane < hi)


def _values_with_ones(vt, hh):
    return jnp.concatenate([vt[hh * HEAD_V:(hh + 1) * HEAD_V], jnp.ones((BF16_ROWS, vt.shape[1]), BF16)], axis=0)


def _col_max(s_t):
    part = jnp.max(s_t.reshape(s_t.shape[0] // MAX_SLAB, MAX_SLAB, s_t.shape[1]), axis=0)
    return jnp.max(part, axis=0, keepdims=True)


def _attend(n_items, logits_fn, values_fn, bounded, needed=lambda item, chunk: True):
    stages = [(i, c) for i in range(n_items) for c in range(SEQ // ATT_TK) if needed(i, c)]
    outs = []
    s_next = logits_fn(*stages[0])
    m = acc = None
    for idx, (item, c) in enumerate(stages):
        first = idx == 0 or stages[idx - 1][0] != item
        last = idx + 1 == len(stages) or stages[idx + 1][0] != item
        s_cur = s_next
        if idx + 1 < len(stages):
            s_next = logits_fn(*stages[idx + 1])
        values = values_fn(item)[:, c * ATT_TK:(c + 1) * ATT_TK]
        if bounded:
            pv = jnp.dot(values, jnp.exp2(s_cur).astype(BF16), preferred_element_type=F32)
            acc = pv if first else acc + pv
        else:
            c_max = _col_max(s_cur)
            m_new = c_max if first else jnp.maximum(m, c_max)
            pv = jnp.dot(values, jnp.exp2((s_cur - m_new).astype(BF16)), preferred_element_type=F32)
            acc = pv if first else acc * jnp.exp2(m - m_new) + pv
            m = m_new
        if last:
            outs.append(acc[:HEAD_V] * (1.0 / acc[HEAD_V:HEAD_V + 1]))
    return outs


def _q_rows(j):
    return slice(j * ATT_TQ, (j + 1) * ATT_TQ)


def _k_rows(c):
    return slice(c * ATT_TK, (c + 1) * ATT_TK)


def _strip_window(strip_ref, hh, qblk, c):
    off = (N_QBLK - 1 - qblk) * ATT_TQ + c * ATT_TK
    return strip_ref[hh, off:off + ATT_TK, :]


def _dil_chunk_needed(qblk, c):
    reach = max(window // 2 for window, _ in DIL_PATTERNS)
    q_lo, q_hi = qblk * ATT_TQ, (qblk + 1) * ATT_TQ - 1
    k_lo, k_hi = c * ATT_TK, (c + 1) * ATT_TK - 1
    return max(k_lo - q_hi, q_lo - k_hi, 0) <= reach


def _masked_q(q_ref, j, lo, hi):
    q = q_ref[0, _q_rows(j), :]
    return jnp.where(_lanes_between(q.shape, lo, hi), q, jnp.zeros_like(q))


def _diff_kernel(q_ref, k_ref, vt_ref, strip_ref, lam_ref, g_ref, o_ref, *, lambda_init, bounded):
    vt = vt_ref[0]
    lp = lam_ref[...]
    lam = (jnp.exp(jnp.sum(lp[0:1] * lp[1:2], axis=-1, keepdims=True))
           - jnp.exp(jnp.sum(lp[2:3] * lp[3:4], axis=-1, keepdims=True)) + lambda_init)
    subs = 2 * 2
    v_ext = [_values_with_ones(vt, hh) for hh in range(2)]

    def logits(i, c):
        j, sub = divmod(i, subs)
        qm = _masked_q(q_ref, j, DIFF_QK_DIM * sub, DIFF_QK_DIM * (sub + 1))
        return (lax.dot_general(k_ref[0, _k_rows(c), :], qm, _NT, preferred_element_type=F32)
                + _strip_window(strip_ref, sub // 2, j, c))

    parts = _attend(N_QBLK * subs, logits, lambda i: v_ext[(i % subs) // 2], bounded)
    for j in range(N_QBLK):
        outs = []
        for hh in range(2):
            comb = parts[j * subs + 2 * hh] - lam * parts[j * subs + 2 * hh + 1]
            ss = jnp.sum(comb * comb, axis=0, keepdims=True)
            outs.append(comb * lax.rsqrt(ss * (1.0 / DIFF_V_DIM) + EPS))
        y = jnp.concatenate(outs, axis=0).T * g_ref[...] * (1.0 - lambda_init)
        o_ref[0, _q_rows(j), :] = y.astype(BF16)


def _dil_kernel(q_ref, k_ref, vt_ref, strip_ref, o_ref, *, bounded):
    vt = vt_ref[0]
    v_ext = [_values_with_ones(vt, hh) for hh in range(2)]

    def logits(i, c):
        j, hh = divmod(i, 2)
        qm = _masked_q(q_ref, j, DIL_HEAD_DIM * hh, DIL_HEAD_DIM * (hh + 1))
        return (lax.dot_general(k_ref[0, _k_rows(c), :], qm, _NT, preferred_element_type=F32)
                + _strip_window(strip_ref, hh, j, c))

    outs = _attend(N_QBLK * 2, logits, lambda i: v_ext[i % 2], bounded,
                   needed=lambda i, c: _dil_chunk_needed(i // 2, c))
    for j in range(N_QBLK):
        o_ref[0, _q_rows(j), :] = jnp.concatenate(outs[2 * j:2 * j + 2], axis=0).T.astype(BF16)


def _mla_kernel(q_ref, k_ref, vt_ref, o_ref, *, bounded):
    vt = vt_ref[0]
    v_ext = [_values_with_ones(vt, hh) for hh in range(2)]

    def logits(i, c):
        j, hh = divmod(i, 2)
        return lax.dot_general(k_ref[0, _k_rows(c), hh * LANE:(hh + 1) * LANE],
                               q_ref[0, _q_rows(j), hh * LANE:(hh + 1) * LANE], _NT,
                               preferred_element_type=F32)

    outs = _attend(N_QBLK * 2, logits, lambda i: v_ext[i % 2], bounded)
    for j in range(N_QBLK):
        o_ref[0, _q_rows(j), :] = jnp.concatenate(outs[2 * j:2 * j + 2], axis=0).T.astype(BF16)


def _attention(kernel, name, bounded, q, k, vt, qk_lanes, strips=None, head0=0, extra=()):
    b, s, _ = q.shape
    n_pairs = vt.shape[1] // LANE
    in_specs = [
        pl.BlockSpec((1, s, qk_lanes), lambda p, bb: (bb, 0, p)),
        pl.BlockSpec((1, s, qk_lanes), lambda p, bb: (bb, 0, p)),
        pl.BlockSpec((1, LANE, s), lambda p, bb: (bb, p, 0)),
    ]
    args = [q, k, vt]
    if strips is not None:
        in_specs.append(pl.BlockSpec((2, STRIP_ROWS, ATT_TQ), lambda p, bb: (head0 // 2 + p, 0, 0),
                                     pipeline_mode=pl.Buffered(1)))
        args.append(strips)
    for a in extra:
        per_pair = a.shape[-1] == n_pairs * LANE
        in_specs.append(pl.BlockSpec((1, LANE), lambda p, bb: (0, p)) if per_pair
                        else pl.BlockSpec(a.shape, lambda p, bb: (0, 0)))
        args.append(a)

    return pl.pallas_call(
        functools.partial(kernel, bounded=bounded),
        name=name + ("_bounded" if bounded else "_online"),
        grid=(n_pairs, b),
        in_specs=in_specs,
        out_specs=pl.BlockSpec((1, s, LANE), lambda p, bb: (bb, 0, p)),
        out_shape=jax.ShapeDtypeStruct((b, s, n_pairs * LANE), BF16),
        compiler_params=_params("parallel", "parallel"),
    )(*args)


def _mixers(bounded, lambda_init):
    def run(dq, dk, dvt, lq, lk, lvt, mq, mk, mvt, strips, lam, subln):
        out_a = _attention(functools.partial(_diff_kernel, lambda_init=lambda_init), "diff_attn", bounded,
                           dq, dk, dvt, LANE, strips, 0, extra=(lam, subln))
        out_b = _attention(_dil_kernel, "dil_attn", bounded, lq, lk, lvt, LANE, strips, DIFF_HEADS)
        out_c = _attention(_mla_kernel, "mla_attn", bounded, mq, mk, mvt, 2 * LANE)
        return out_a, out_b, out_c
    return run


def _logit_bound(gain_q, gain_k, dim, bias_abs=0.0):
    return dim * jnp.max(jnp.abs(gain_q)) * jnp.max(jnp.abs(gain_k)) + bias_abs


def _rope_tables():
    half = MLA_ROPE_DIM // 2
    inv = ROPE_THETA ** (-jnp.arange(half, dtype=F32) / half)
    ang = jnp.arange(SEQ).astype(F32)[:, None] * inv[None, :]
    cos, sin = jnp.cos(ang), jnp.sin(ang)
    ones = jnp.ones((SEQ, MLA_NOPE_DIM), F32)
    tail = LANE - MLA_QK_DIM
    cos_t = jnp.concatenate([ones, cos, cos, jnp.ones((SEQ, tail), F32)], axis=1)
    sin_t = jnp.concatenate([0.0 * ones, -sin, sin, jnp.zeros((SEQ, tail), F32)], axis=1)
    return cos_t, sin_t


def _proj_weights(l, mix_norm, w_in, diff_q_norm, diff_k_norm, dil_q_norm, dil_k_norm,
                  mla_q_norm, mla_q_up, mla_kv_norm, mla_kv_up, mla_qn, mla_kn, cos_t, sin_t):
    wi = w_in[l]
    pad_to_block = LANE - MLA_QK_DIM
    w_ext = jnp.concatenate([wi[:, :SEG_ROPE], jnp.zeros((D_MODEL, MLA_NOPE_DIM), F32),
                             wi[:, SEG_ROPE:], jnp.zeros((D_MODEL, pad_to_block), F32)], axis=1)
    q_up = jnp.pad(mla_q_up[l].reshape(MLA_Q_RANK, MLA_HEADS, MLA_QK_DIM), ((0, 0), (0, 0), (0, pad_to_block)))
    kv = mla_kv_up[l].reshape(MLA_KV_RANK, MLA_HEADS, MLA_NOPE_DIM + MLA_V_DIM)
    kv_k = jnp.pad(kv[:, :, :MLA_NOPE_DIM], ((0, 0), (0, 0), (0, LANE - MLA_NOPE_DIM)))
    kv_v = kv[:, :, MLA_NOPE_DIM:]

    def head_gain(g, heads, scale=1.0, pad=0):
        return jnp.tile(jnp.pad(g.astype(F32) * scale, (0, pad)), heads)[None, :]

    return {
        "g": mix_norm[l][None, :],
        "w_in": w_ext.astype(BF16),
        "q_up": q_up.reshape(MLA_Q_RANK, MLA_PAD).astype(BF16),
        "kv_k": kv_k.reshape(MLA_KV_RANK, MLA_PAD).astype(BF16),
        "kv_v": kv_v.reshape(MLA_KV_RANK, MLA_WIDTH).astype(BF16),
        "gdq": head_gain(diff_q_norm[l], 2 * DIFF_HEADS, DIFF_QK_DIM ** -0.5 * LOG2E),
        "gdk": head_gain(diff_k_norm[l], 2 * DIFF_HEADS),
        "glq": head_gain(dil_q_norm[l], DIL_HEADS, DIL_HEAD_DIM ** -0.5 * LOG2E),
        "glk": head_gain(dil_k_norm[l], DIL_HEADS),
        "gmq": mla_q_norm[l][None, :],
        "gmkv": mla_kv_norm[l][None, :],
        "gqn": head_gain(mla_qn[l], MLA_HEADS, MLA_QK_DIM ** -0.5 * LOG2E, pad_to_block),
        "gkn": head_gain(mla_kn[l], MLA_HEADS, 1.0, pad_to_block),
        "cos": cos_t,
        "sin": sin_t,
    }


def kernel(x, rel_bias, ffn1_norm, ffn1_wg, ffn1_wu, ffn1_wd, mix_norm, w_in, diff_q_norm, diff_k_norm, diff_lambda, diff_subln, dil_q_norm, dil_k_norm, mla_q_norm, mla_q_up, mla_kv_norm, mla_kv_up, mla_qn, mla_kn, w_o, ffn2_norm, ffn2_wg, ffn2_wu, ffn2_wd):
    b, s, d = x.shape
    assert (s, d) == (SEQ, D_MODEL)
    t = b * s
    strips = _bias_strips(_strip_sources(rel_bias))
    cos_t, sin_t = _rope_tables()
    x2d = x.reshape(t, d)
    ffn1 = [_cast_stack(w) for w in (ffn1_wg, ffn1_wu, ffn1_wd)]
    ffn2 = [_cast_stack(w) for w in (ffn2_wg, ffn2_wu, ffn2_wd)]
    w_o_bf16 = _cast_stack(w_o)
    for l in range(DEPTH):
        lambda_init = 0.8 - 0.6 * math.exp(-0.3 * (l + 1))
        x2d = _ffn(x2d, l, ffn1_norm, *ffn1)
        pw = _proj_weights(l, mix_norm, w_in, diff_q_norm, diff_k_norm, dil_q_norm, dil_k_norm,
                           mla_q_norm, mla_q_up, mla_kv_norm, mla_kv_up, mla_qn, mla_kn, cos_t, sin_t)
        dq, dk, dvt, lq, lk, lvt, mq, mk, mvt = _proj(x2d, pw, b)
        tok = lambda a: a.reshape(b, s, -1)
        subln = jnp.tile(diff_subln[l].astype(F32), DIFF_HEADS)[None, :]
        bias_abs = jnp.max(jnp.abs(rel_bias.astype(F32)), axis=0) * LOG2E
        bound_a = _logit_bound(pw["gdq"], pw["gdk"], DIFF_QK_DIM, jnp.max(bias_abs[:DIFF_HEADS]))
        bound_b = _logit_bound(pw["glq"], pw["glk"], DIL_HEAD_DIM,
                               jnp.max(bias_abs[DIFF_HEADS:]) + math.log2(len(DIL_PATTERNS)))
        bound_c = _logit_bound(pw["gqn"], pw["gkn"], MLA_QK_DIM)
        logits_bounded = jnp.maximum(jnp.maximum(bound_a, bound_b), bound_c) <= LOGIT_BOUND
        out_a, out_b, out_c = lax.cond(
            logits_bounded, _mixers(True, lambda_init), _mixers(False, lambda_init),
            tok(dq), tok(dk), dvt, tok(lq), tok(lk), lvt, tok(mq), tok(mk), mvt, strips,
            diff_lambda[l].astype(F32), subln)
        mix = (out_a.reshape(t, -1), out_b.reshape(t, -1), out_c.reshape(t, -1), w_o_bf16)
        x2d = _ffn(x2d, l, ffn2_norm, *ffn2, mix=mix)
    return x2d.reshape(b, s, d)
```

```python
import functools
import math

import jax
import jax.numpy as jnp
import numpy as np
from jax import lax
from jax.experimental import pallas as pl
from jax.experimental.pallas import tpu as pltpu

F32 = jnp.float32
BF16 = jnp.bfloat16

D_MODEL = 1024
SEQ = 2048
DEPTH = 2
EPS = 1e-6
D_FF = 2816
NUM_BUCKETS = 32
MAX_DISTANCE = 1024
ROPE_THETA = 10000.0
NEG_BIG = -1e30
LOG2E = math.log2(math.e)
DIFF_HEADS, DIFF_QK_DIM, DIFF_V_DIM = 4, 32, 64
DIL_HEADS, DIL_HEAD_DIM = 6, 64
DIL_PATTERNS = ((128, 1), (512, 4), (2048, 16))
MLA_HEADS, MLA_Q_RANK, MLA_KV_RANK = 6, 256, 128
MLA_NOPE_DIM, MLA_ROPE_DIM, MLA_V_DIM = 64, 32, 64
MLA_QK_DIM = MLA_NOPE_DIM + MLA_ROPE_DIM
HEAD_V = 64
DIFF_WIDTH = DIFF_HEADS * DIFF_V_DIM
DIL_WIDTH = DIL_HEADS * DIL_HEAD_DIM
MLA_WIDTH = MLA_HEADS * MLA_V_DIM
BIAS_HEADS = DIFF_HEADS + DIL_HEADS

LANE = 128
SUBLANE = 8
BF16_ROWS = 16
VMEM_LIMIT_BYTES = 56 * 1024 * 1024

FFN_TM = 512
CAST_BLOCKS = 4
PROJ_TM = 512
PROJ_SPLIT = 2
ATT_TQ = 512
ATT_TK = 512
MAX_SLAB = 128
LOGIT_BOUND = 50.0
STRIP_ROWS = 2 * SEQ - ATT_TQ
STRIP_SRC_W = 2 * SEQ + 2 * LANE
N_QBLK = SEQ // ATT_TQ

SEG_DQ, SEG_DK, SEG_DV = 0, 256, 512
SEG_LQ, SEG_LK, SEG_LV = 768, 1152, 1536
SEG_MQ, SEG_MKV, SEG_ROPE = 1920, 2176, 2304
IN_EXT = 2432
MLA_PAD = MLA_HEADS * LANE


def _params(*sem):
    return pltpu.CompilerParams(dimension_semantics=sem, vmem_limit_bytes=VMEM_LIMIT_BYTES)


def _rms(x, g):
    return x * lax.rsqrt(jnp.mean(x * x, axis=-1, keepdims=True) + EPS) * g


def _cast_kernel(w_ref, o_ref):
    o_ref[...] = w_ref[...].astype(BF16)


def _cast_stack(w):
    layers, rows, cols = w.shape
    blk = pl.BlockSpec((1, rows // CAST_BLOCKS, cols), lambda l, r: (l, r, 0))
    return pl.pallas_call(
        _cast_kernel,
        name="cast_bf16",
        grid=(layers, CAST_BLOCKS),
        in_specs=[blk],
        out_specs=blk,
        out_shape=jax.ShapeDtypeStruct(w.shape, BF16),
        compiler_params=_params("parallel", "parallel"),
    )(w)


def _ffn_kernel(*refs, fused_out):
    if fused_out:
        x_ref, a_ref, b_ref, c_ref, wo_ref, g_ref, wg_ref, wu_ref, wd_ref, o_ref = refs
        mix_refs = (a_ref, b_ref, c_ref)
    else:
        x_ref, g_ref, wg_ref, wu_ref, wd_ref, o_ref = refs
        mix_refs = ()
    x = x_ref[...]
    row = 0
    for part in mix_refs:
        width = part.shape[1]
        x += jnp.dot(part[...], wo_ref[row:row + width, :], preferred_element_type=F32)
        row += width
    n = _rms(x, g_ref[...]).astype(BF16)
    gate = jnp.dot(n, wg_ref[...], preferred_element_type=F32)
    up = jnp.dot(n, wu_ref[...], preferred_element_type=F32)
    h = (gate * jax.nn.sigmoid(gate) * up).astype(BF16)
    o_ref[...] = x + 0.5 * jnp.dot(h, wd_ref[...], preferred_element_type=F32)


def _ffn(x2d, layer, g, wg, wu, wd, mix=None):
    t = x2d.shape[0]
    tm = FFN_TM
    row = lambda i: (i, 0)

    def resident(a):
        return pl.BlockSpec((None,) + a.shape[1:], lambda i: (layer, 0, 0), pipeline_mode=pl.Buffered(1))

    args, in_specs = [x2d], [pl.BlockSpec((tm, D_MODEL), row)]
    if mix is not None:
        a, b, c, wo = mix
        args += [a, b, c, wo]
        in_specs += [pl.BlockSpec((tm, m.shape[1]), row) for m in (a, b, c)] + [resident(wo)]
    weights = [g[:, None, :], wg, wu, wd]
    return pl.pallas_call(
        functools.partial(_ffn_kernel, fused_out=mix is not None),
        name="ffn_out" if mix is not None else "ffn",
        grid=(t // tm,),
        in_specs=in_specs + [resident(w) for w in weights],
        out_specs=pl.BlockSpec((tm, D_MODEL), row),
        out_shape=jax.ShapeDtypeStruct((t, D_MODEL), F32),
        compiler_params=_params("parallel"),
    )(*args, *weights)


def _group_norm(x, gsum, inv_n, gain):
    ss = jnp.dot((x * x).astype(BF16), gsum, preferred_element_type=F32)
    return x * lax.rsqrt(ss * inv_n + EPS) * gain


def _group_norm_chunks(x, gsum, inv_n, gain):
    w = gsum.shape[0]
    parts = [_group_norm(x[:, c:c + w], gsum, inv_n, gain[:, c:c + w]) for c in range(0, x.shape[1], w)]
    return jnp.concatenate(parts, axis=1)


def _rope_blocks(x, cos, sin):
    lane = lax.broadcasted_iota(jnp.int32, (x.shape[0], LANE), 1)
    first_half = lane < MLA_NOPE_DIM + MLA_ROPE_DIM // 2
    parts = []
    for c in range(0, x.shape[1], LANE):
        xb = x[:, c:c + LANE]
        partner = jnp.where(first_half,
                            pltpu.roll(xb, LANE - MLA_ROPE_DIM // 2, 1),
                            pltpu.roll(xb, MLA_ROPE_DIM // 2, 1))
        parts.append(xb * cos + partner * sin)
    return jnp.concatenate(parts, axis=1)


def _proj_kernel(x_ref, g_ref, win_ref, qup_ref, kvk_ref, kvv_ref,
                 gdq_ref, gdk_ref, glq_ref, glk_ref, gmq_ref, gmkv_ref, gqn_ref, gkn_ref,
                 cos_ref, sin_ref, g32_ref, g64_ref, g128_ref,
                 dq_ref, dk_ref, dvt_ref, lq_ref, lk_ref, lvt_ref, mq_ref, mk_ref, mvt_ref):
    g32, g64, g128 = g32_ref[...], g64_ref[...], g128_ref[...]
    sub = PROJ_TM // PROJ_SPLIT
    for r in range(PROJ_SPLIT):
        rows = slice(r * sub, (r + 1) * sub)
        h = _rms(x_ref[rows, :], g_ref[...]).astype(BF16)
        proj = jnp.dot(h, win_ref[...], preferred_element_type=F32)

        def seg(lo, hi):
            return proj[:, lo:hi]

        dq_ref[rows, :] = _group_norm(seg(SEG_DQ, SEG_DK), g32, 1.0 / DIFF_QK_DIM, gdq_ref[...]).astype(BF16)
        dk_ref[rows, :] = _group_norm(seg(SEG_DK, SEG_DV), g32, 1.0 / DIFF_QK_DIM, gdk_ref[...]).astype(BF16)
        dvt_ref[0, :, rows] = seg(SEG_DV, SEG_LQ).T.astype(BF16)
        lqk = _group_norm_chunks(seg(SEG_LQ, SEG_LV), g64, 1.0 / DIL_HEAD_DIM,
                                 jnp.concatenate([glq_ref[...], glk_ref[...]], axis=1))
        lq_ref[rows, :] = lqk[:, :DIL_WIDTH].astype(BF16)
        lk_ref[rows, :] = lqk[:, DIL_WIDTH:].astype(BF16)
        lvt_ref[0, :, rows] = seg(SEG_LV, SEG_MQ).T.astype(BF16)

        cos, sin = cos_ref[rows, :], sin_ref[rows, :]
        q_lat = _rms(seg(SEG_MQ, SEG_MKV), gmq_ref[...]).astype(BF16)
        q = jnp.dot(q_lat, qup_ref[...], preferred_element_type=F32)
        q = _group_norm_chunks(q, g128, 1.0 / MLA_QK_DIM, gqn_ref[...])
        mq_ref[rows, :] = _rope_blocks(q, cos, sin).astype(BF16)

        c_kv = _rms(seg(SEG_MKV, SEG_ROPE), gmkv_ref[...]).astype(BF16)
        k_rope = seg(SEG_ROPE, IN_EXT)
        k = jnp.dot(c_kv, kvk_ref[...], preferred_element_type=F32)
        k = k + jnp.concatenate([k_rope] * MLA_HEADS, axis=1)
        k = _group_norm_chunks(k, g128, 1.0 / MLA_QK_DIM, gkn_ref[...])
        mk_ref[rows, :] = _rope_blocks(k, cos, sin).astype(BF16)
        mvt_ref[0, :, rows] = jnp.dot(c_kv, kvv_ref[...], preferred_element_type=F32).T.astype(BF16)


def _block_diag_ones(width, group):
    idx = np.arange(width) // group
    return jnp.asarray((idx[:, None] == idx[None, :]).astype(np.float32), dtype=BF16)


def _proj(x2d, pw, batch):
    t = x2d.shape[0]
    tm = PROJ_TM
    row = lambda i: (i, 0)
    fixed = lambda i: (0, 0)
    pos_blocks = SEQ // tm
    col = lambda i: (i // pos_blocks, 0, i % pos_blocks)

    def full(a):
        return pl.BlockSpec(a.shape, fixed)

    consts = [pw["g"], pw["w_in"], pw["q_up"], pw["kv_k"], pw["kv_v"],
              pw["gdq"], pw["gdk"], pw["glq"], pw["glk"], pw["gmq"], pw["gmkv"], pw["gqn"], pw["gkn"]]
    tables = [pw["cos"], pw["sin"]]
    gsums = [_block_diag_ones(256, DIFF_QK_DIM), _block_diag_ones(256, DIL_HEAD_DIM),
             _block_diag_ones(256, LANE)]
    outs = [(256, False), (256, False), (DIFF_WIDTH, True), (DIL_WIDTH, False), (DIL_WIDTH, False),
            (DIL_WIDTH, True), (MLA_PAD, False), (MLA_PAD, False), (MLA_WIDTH, True)]
    return pl.pallas_call(
        _proj_kernel,
        name="proj",
        grid=(t // tm,),
        in_specs=([pl.BlockSpec((tm, D_MODEL), row)] + [full(a) for a in consts]
                  + [pl.BlockSpec((tm, LANE), lambda i: (i % pos_blocks, 0)) for _ in tables]
                  + [full(a) for a in gsums]),
        out_specs=[pl.BlockSpec((1, w, tm), col) if tr else pl.BlockSpec((tm, w), row) for w, tr in outs],
        out_shape=[jax.ShapeDtypeStruct((batch, w, SEQ) if tr else (t, w), BF16) for w, tr in outs],
        compiler_params=_params("parallel"),
    )(x2d, *consts, *tables, *gsums)


def _strip_kernel(rev_ref, o_ref, src_scr):
    row = rev_ref[0]
    for s in range(SUBLANE):
        shift = SUBLANE - 1 - s
        src_scr[s:s + 1, :] = pltpu.roll(row, STRIP_SRC_W - shift, 1) if shift else row
    chunk_w = ATT_TQ + LANE
    for a in range(STRIP_ROWS // SUBLANE):
        start = STRIP_ROWS - SUBLANE - SUBLANE * a
        lo = (start // LANE) * LANE
        chunk = src_scr[:, lo:lo + chunk_w]
        shift = start - lo
        if shift:
            chunk = pltpu.roll(chunk, chunk_w - shift, 1)
        o_ref[0, a * SUBLANE:(a + 1) * SUBLANE, :] = chunk[:, :ATT_TQ]


def _bias_strips(rev):
    nh = rev.shape[0]
    return pl.pallas_call(
        _strip_kernel,
        name="bias_strips",
        grid=(nh,),
        in_specs=[pl.BlockSpec((1, 1, STRIP_SRC_W), lambda h: (h, 0, 0))],
        out_specs=pl.BlockSpec((1, STRIP_ROWS, ATT_TQ), lambda h: (h, 0, 0)),
        out_shape=jax.ShapeDtypeStruct((nh, STRIP_ROWS, ATT_TQ), F32),
        scratch_shapes=[pltpu.VMEM((SUBLANE, STRIP_SRC_W), F32)],
        compiler_params=_params("parallel"),
    )(rev)


def _t5_bucket(rel):
    half = NUM_BUCKETS // 2
    max_exact = half // 2
    n = jnp.abs(rel)
    nf = jnp.maximum(n, 1).astype(F32)
    large = max_exact + (jnp.log(nf / max_exact) / math.log(MAX_DISTANCE / max_exact)
                         * (half - max_exact)).astype(jnp.int32)
    large = jnp.minimum(large, half - 1)
    return jnp.where(rel > 0, half, 0) + jnp.where(n < max_exact, n, large)


def _strip_sources(rel_bias):
    rel = np.arange(-(SEQ - 1), SEQ)
    mult = np.zeros(rel.shape, np.int64)
    for window, dil in DIL_PATTERNS:
        mult += (rel % dil == 0) & (np.abs(rel) // dil <= window // (2 * dil))
    bucket = _t5_bucket(jnp.asarray(rel, jnp.int32))
    one_hot = (bucket[None, :] == jnp.arange(NUM_BUCKETS)[:, None]).astype(F32)
    table = jnp.dot(rel_bias.astype(F32).T, one_hot, precision=lax.Precision.HIGHEST)
    is_dil = jnp.asarray(np.arange(BIAS_HEADS) >= DIFF_HEADS)[:, None]
    logm = jnp.asarray(np.log(np.maximum(mult, 1)), F32)[None, :]
    reachable = jnp.asarray(mult > 0)[None, :]
    vec = jnp.where(is_dil & ~reachable, NEG_BIG, table + jnp.where(is_dil, logm, 0.0)) * LOG2E
    rev = jnp.pad(vec[:, ::-1], ((0, 0), (0, STRIP_SRC_W - vec.shape[1])))
    return rev[:, None, :]


_NT = (((1,), (1,)), ((), ()))


def _lanes_between(shape, lo, hi):
    lane = lax.broadcasted_iota(jnp.int32, shape, 1)
    return (lane >= lo) & (lane < hi)


def _values_with_ones(vt, hh):
    return jnp.concatenate([vt[hh * HEAD_V:(hh + 1) * HEAD_V], jnp.ones((BF16_ROWS, vt.shape[1]), BF16)], axis=0)


def _col_max(s_t):
    part = jnp.max(s_t.reshape(s_t.shape[0] // MAX_SLAB, MAX_SLAB, s_t.shape[1]), axis=0)
    return jnp.max(part, axis=0, keepdims=True)


def _attend(n_items, logits_fn, values_fn, bounded, needed=lambda item, chunk: True):
    stages = [(i, c) for i in range(n_items) for c in range(SEQ // ATT_TK) if needed(i, c)]
    outs = []
    s_next = logits_fn(*stages[0])
    m = acc = None
    for idx, (item, c) in enumerate(stages):
        first = idx == 0 or stages[idx - 1][0] != item
        last = idx + 1 == len(stages) or stages[idx + 1][0] != item
        s_cur = s_next
        if idx + 1 < len(stages):
            s_next = logits_fn(*stages[idx + 1])
        values = values_fn(item)[:, c * ATT_TK:(c + 1) * ATT_TK]
        if bounded:
            pv = jnp.dot(values, jnp.exp2(s_cur).astype(BF16), preferred_element_type=F32)
            acc = pv if first else acc + pv
        else:
            c_max = _col_max(s_cur)
            m_new = c_max if first else jnp.maximum(m, c_max)
            pv = jnp.dot(values, jnp.exp2((s_cur - m_new).astype(BF16)), preferred_element_type=F32)
            acc = pv if first else acc * jnp.exp2(m - m_new) + pv
            m = m_new
        if last:
            outs.append(acc[:HEAD_V] * (1.0 / acc[HEAD_V:HEAD_V + 1]))
    return outs


def _q_rows(j):
    return slice(j * ATT_TQ, (j + 1) * ATT_TQ)


def _k_rows(c):
    return slice(c * ATT_TK, (c + 1) * ATT_TK)


def _strip_window(strip_ref, hh, qblk, c):
    off = (N_QBLK - 1 - qblk) * ATT_TQ + c * ATT_TK
    return strip_ref[hh, off:off + ATT_TK, :]


def _dil_chunk_needed(qblk, c):
    reach = max(window // 2 for window, _ in DIL_PATTERNS)
    q_lo, q_hi = qblk * ATT_TQ, (qblk + 1) * ATT_TQ - 1
    k_lo, k_hi = c * ATT_TK, (c + 1) * ATT_TK - 1
    return max(k_lo - q_hi, q_lo - k_hi, 0) <= reach


def _masked_q(q_ref, j, lo, hi):
    q = q_ref[0, _q_rows(j), :]
    return jnp.where(_lanes_between(q.shape, lo, hi), q, jnp.zeros_like(q))


def _diff_kernel(q_ref, k_ref, vt_ref, strip_ref, lam_ref, g_ref, o_ref, *, lambda_init, bounded):
    vt = vt_ref[0]
    lp = lam_ref[...]
    lam = (jnp.exp(jnp.sum(lp[0:1] * lp[1:2], axis=-1, keepdims=True))
           - jnp.exp(jnp.sum(lp[2:3] * lp[3:4], axis=-1, keepdims=True)) + lambda_init)
    subs = 2 * 2
    v_ext = [_values_with_ones(vt, hh) for hh in range(2)]

    def logits(i, c):
        j, sub = divmod(i, subs)
        qm = _masked_q(q_ref, j, DIFF_QK_DIM * sub, DIFF_QK_DIM * (sub + 1))
        return (lax.dot_general(k_ref[0, _k_rows(c), :], qm, _NT, preferred_element_type=F32)
                + _strip_window(strip_ref, sub // 2, j, c))

    parts = _attend(N_QBLK * subs, logits, lambda i: v_ext[(i % subs) // 2], bounded)
    for j in range(N_QBLK):
        outs = []
        for hh in range(2):
            comb = parts[j * subs + 2 * hh] - lam * parts[j * subs + 2 * hh + 1]
            ss = jnp.sum(comb * comb, axis=0, keepdims=True)
            outs.append(comb * lax.rsqrt(ss * (1.0 / DIFF_V_DIM) + EPS))
        y = jnp.concatenate(outs, axis=0).T * g_ref[...] * (1.0 - lambda_init)
        o_ref[0, _q_rows(j), :] = y.astype(BF16)


def _dil_kernel(q_ref, k_ref, vt_ref, strip_ref, o_ref, *, bounded):
    vt = vt_ref[0]
    v_ext = [_values_with_ones(vt, hh) for hh in range(2)]

    def logits(i, c):
        j, hh = divmod(i, 2)
        qm = _masked_q(q_ref, j, DIL_HEAD_DIM * hh, DIL_HEAD_DIM * (hh + 1))
        return (lax.dot_general(k_ref[0, _k_rows(c), :], qm, _NT, preferred_element_type=F32)
                + _strip_window(strip_ref, hh, j, c))

    outs = _attend(N_QBLK * 2, logits, lambda i: v_ext[i % 2], bounded,
                   needed=lambda i, c: _dil_chunk_needed(i // 2, c))
    for j in range(N_QBLK):
        o_ref[0, _q_rows(j), :] = jnp.concatenate(outs[2 * j:2 * j + 2], axis=0).T.astype(BF16)


def _mla_kernel(q_ref, k_ref, vt_ref, o_ref, *, bounded):
    vt = vt_ref[0]
    v_ext = [_values_with_ones(vt, hh) for hh in range(2)]

    def logits(i, c):
        j, hh = divmod(i, 2)
        return lax.dot_general(k_ref[0, _k_rows(c), hh * LANE:(hh + 1) * LANE],
                               q_ref[0, _q_rows(j), hh * LANE:(hh + 1) * LANE], _NT,
                               preferred_element_type=F32)

    outs = _attend(N_QBLK * 2, logits, lambda i: v_ext[i % 2], bounded)
    for j in range(N_QBLK):
        o_ref[0, _q_rows(j), :] = jnp.concatenate(outs[2 * j:2 * j + 2], axis=0).T.astype(BF16)


def _attention(kernel, name, bounded, q, k, vt, qk_lanes, strips=None, head0=0, extra=()):
    b, s, _ = q.shape
    n_pairs = vt.shape[1] // LANE
    in_specs = [
        pl.BlockSpec((1, s, qk_lanes), lambda p, bb: (bb, 0, p)),
        pl.BlockSpec((1, s, qk_lanes), lambda p, bb: (bb, 0, p)),
        pl.BlockSpec((1, LANE, s), lambda p, bb: (bb, p, 0)),
    ]
    args = [q, k, vt]
    if strips is not None:
        in_specs.append(pl.BlockSpec((2, STRIP_ROWS, ATT_TQ), lambda p, bb: (head0 // 2 + p, 0, 0),
                                     pipeline_mode=pl.Buffered(1)))
        args.append(strips)
    for a in extra:
        per_pair = a.shape[-1] == n_pairs * LANE
        in_specs.append(pl.BlockSpec((1, LANE), lambda p, bb: (0, p)) if per_pair
                        else pl.BlockSpec(a.shape, lambda p, bb: (0, 0)))
        args.append(a)

    return pl.pallas_call(
        functools.partial(kernel, bounded=bounded),
        name=name + ("_bounded" if bounded else "_online"),
        grid=(n_pairs, b),
        in_specs=in_specs,
        out_specs=pl.BlockSpec((1, s, LANE), lambda p, bb: (bb, 0, p)),
        out_shape=jax.ShapeDtypeStruct((b, s, n_pairs * LANE), BF16),
        compiler_params=_params("parallel", "parallel"),
    )(*args)


def _mixers(bounded, lambda_init):
    def run(dq, dk, dvt, lq, lk, lvt, mq, mk, mvt, strips, lam, subln):
        out_a = _attention(functools.partial(_diff_kernel, lambda_init=lambda_init), "diff_attn", bounded,
                           dq, dk, dvt, LANE, strips, 0, extra=(lam, subln))
        out_b = _attention(_dil_kernel, "dil_attn", bounded, lq, lk, lvt, LANE, strips, DIFF_HEADS)
        out_c = _attention(_mla_kernel, "mla_attn", bounded, mq, mk, mvt, 2 * LANE)
        return out_a, out_b, out_c
    return run


def _logit_bound(gain_q, gain_k, dim, bias_abs=0.0):
    return dim * jnp.max(jnp.abs(gain_q)) * jnp.max(jnp.abs(gain_k)) + bias_abs


def _rope_tables():
    half = MLA_ROPE_DIM // 2
    inv = ROPE_THETA ** (-jnp.arange(half, dtype=F32) / half)
    ang = jnp.arange(SEQ).astype(F32)[:, None] * inv[None, :]
    cos, sin = jnp.cos(ang), jnp.sin(ang)
    ones = jnp.ones((SEQ, MLA_NOPE_DIM), F32)
    tail = LANE - MLA_QK_DIM
    cos_t = jnp.concatenate([ones, cos, cos, jnp.ones((SEQ, tail), F32)], axis=1)
    sin_t = jnp.concatenate([0.0 * ones, -sin, sin, jnp.zeros((SEQ, tail), F32)], axis=1)
    return cos_t, sin_t


def _proj_weights(l, mix_norm, w_in, diff_q_norm, diff_k_norm, dil_q_norm, dil_k_norm,
                  mla_q_norm, mla_q_up, mla_kv_norm, mla_kv_up, mla_qn, mla_kn, cos_t, sin_t):
    wi = w_in[l]
    pad_to_block = LANE - MLA_QK_DIM
    w_ext = jnp.concatenate([wi[:, :SEG_ROPE], jnp.zeros((D_MODEL, MLA_NOPE_DIM), F32),
                             wi[:, SEG_ROPE:], jnp.zeros((D_MODEL, pad_to_block), F32)], axis=1)
    q_up = jnp.pad(mla_q_up[l].reshape(MLA_Q_RANK, MLA_HEADS, MLA_QK_DIM), ((0, 0), (0, 0), (0, pad_to_block)))
    kv = mla_kv_up[l].reshape(MLA_KV_RANK, MLA_HEADS, MLA_NOPE_DIM + MLA_V_DIM)
    kv_k = jnp.pad(kv[:, :, :MLA_NOPE_DIM], ((0, 0), (0, 0), (0, LANE - MLA_NOPE_DIM)))
    kv_v = kv[:, :, MLA_NOPE_DIM:]

    def head_gain(g, heads, scale=1.0, pad=0):
        return jnp.tile(jnp.pad(g.astype(F32) * scale, (0, pad)), heads)[None, :]

    return {
        "g": mix_norm[l][None, :],
        "w_in": w_ext.astype(BF16),
        "q_up": q_up.reshape(MLA_Q_RANK, MLA_PAD).astype(BF16),
        "kv_k": kv_k.reshape(MLA_KV_RANK, MLA_PAD).astype(BF16),
        "kv_v": kv_v.reshape(MLA_KV_RANK, MLA_WIDTH).astype(BF16),
        "gdq": head_gain(diff_q_norm[l], 2 * DIFF_HEADS, DIFF_QK_DIM ** -0.5 * LOG2E),
        "gdk": head_gain(diff_k_norm[l], 2 * DIFF_HEADS),
        "glq": head_gain(dil_q_norm[l], DIL_HEADS, DIL_HEAD_DIM ** -0.5 * LOG2E),
        "glk": head_gain(dil_k_norm[l], DIL_HEADS),
        "gmq": mla_q_norm[l][None, :],
        "gmkv": mla_kv_norm[l][None, :],
        "gqn": head_gain(mla_qn[l], MLA_HEADS, MLA_QK_DIM ** -0.5 * LOG2E, pad_to_block),
        "gkn": head_gain(mla_kn[l], MLA_HEADS, 1.0, pad_to_block),
        "cos": cos_t,
        "sin": sin_t,
    }


def kernel(x, rel_bias, ffn1_norm, ffn1_wg, ffn1_wu, ffn1_wd, mix_norm, w_in, diff_q_norm, diff_k_norm, diff_lambda, diff_subln, dil_q_norm, dil_k_norm, mla_q_norm, mla_q_up, mla_kv_norm, mla_kv_up, mla_qn, mla_kn, w_o, ffn2_norm, ffn2_wg, ffn2_wu, ffn2_wd):
    b, s, d = x.shape
    assert (s, d) == (SEQ, D_MODEL)
    t = b * s
    strips = _bias_strips(_strip_sources(rel_bias))
    cos_t, sin_t = _rope_tables()
    x2d = x.reshape(t, d)
    ffn1 = [_cast_stack(w) for w in (ffn1_wg, ffn1_wu, ffn1_wd)]
    ffn2 = [_cast_stack(w) for w in (ffn2_wg, ffn2_wu, ffn2_wd)]
    w_o_bf16 = _cast_stack(w_o)
    for l in range(DEPTH):
        lambda_init = 0.8 - 0.6 * math.exp(-0.3 * (l + 1))
        x2d = _ffn(x2d, l, ffn1_norm, *ffn1)
        pw = _proj_weights(l, mix_norm, w_in, diff_q_norm, diff_k_norm, dil_q_norm, dil_k_norm,
                           mla_q_norm, mla_q_up, mla_kv_norm, mla_kv_up, mla_qn, mla_kn, cos_t, sin_t)
        dq, dk, dvt, lq, lk, lvt, mq, mk, mvt = _proj(x2d, pw, b)
        tok = lambda a: a.reshape(b, s, -1)
        subln = jnp.tile(diff_subln[l].astype(F32), DIFF_HEADS)[None, :]
        bias_abs = jnp.max(jnp.abs(rel_bias.astype(F32)), axis=0) * LOG2E
        bound_a = _logit_bound(pw["gdq"], pw["gdk"], DIFF_QK_DIM, jnp.max(bias_abs[:DIFF_HEADS]))
        bound_b = _logit_bound(pw["glq"], pw["glk"], DIL_HEAD_DIM,
                               jnp.max(bias_abs[DIFF_HEADS:]) + math.log2(len(DIL_PATTERNS)))
        bound_c = _logit_bound(pw["gqn"], pw["gkn"], MLA_QK_DIM)
        logits_bounded = jnp.maximum(jnp.maximum(bound_a, bound_b), bound_c) <= LOGIT_BOUND
        out_a, out_b, out_c = lax.cond(
            logits_bounded, _mixers(True, lambda_init), _mixers(False, lambda_init),
            tok(dq), tok(dk), dvt, tok(lq), tok(lk), lvt, tok(mq), tok(mk), mvt, strips,
            diff_lambda[l].astype(F32), subln)
        mix = (out_a.reshape(t, -1), out_b.reshape(t, -1), out_c.reshape(t, -1), w_o_bf16)
        x2d = _ffn(x2d, l, ffn2_norm, *ffn2, mix=mix)
    return x2d.reshape(b, s, d)
```

```python
import functools
import math

import jax
import jax.numpy as jnp
import numpy as np
from jax import lax
from jax.experimental import pallas as pl
from jax.experimental.pallas import tpu as pltpu

F32 = jnp.float32
BF16 = jnp.bfloat16

D_MODEL = 1024
SEQ = 2048
DEPTH = 2
EPS = 1e-6
D_FF = 2816
NUM_BUCKETS = 32
MAX_DISTANCE = 1024
ROPE_THETA = 10000.0
NEG_BIG = -1e30
LOG2E = math.log2(math.e)
DIFF_HEADS, DIFF_QK_DIM, DIFF_V_DIM = 4, 32, 64
DIL_HEADS, DIL_HEAD_DIM = 6, 64
DIL_PATTERNS = ((128, 1), (512, 4), (2048, 16))
MLA_HEADS, MLA_Q_RANK, MLA_KV_RANK = 6, 256, 128
MLA_NOPE_DIM, MLA_ROPE_DIM, MLA_V_DIM = 64, 32, 64
MLA_QK_DIM = MLA_NOPE_DIM + MLA_ROPE_DIM
HEAD_V = 64
DIFF_WIDTH = DIFF_HEADS * DIFF_V_DIM
DIL_WIDTH = DIL_HEADS * DIL_HEAD_DIM
MLA_WIDTH = MLA_HEADS * MLA_V_DIM
BIAS_HEADS = DIFF_HEADS + DIL_HEADS

LANE = 128
SUBLANE = 8
BF16_ROWS = 16
VMEM_LIMIT_BYTES = 56 * 1024 * 1024

FFN_TM = 512
CAST_BLOCKS = 4
STAGE_CHUNKS = 8
PROJ_TM = 512
PROJ_SPLIT = 2
ATT_TQ = 512
ATT_TK = 512
MAX_SLAB = 128
LOGIT_BOUND = 50.0
STRIP_ROWS = 2 * SEQ - ATT_TQ
STRIP_SRC_W = 2 * SEQ + 2 * LANE
N_QBLK = SEQ // ATT_TQ

SEG_DQ, SEG_DK, SEG_DV = 0, 256, 512
SEG_LQ, SEG_LK, SEG_LV = 768, 1152, 1536
SEG_MQ, SEG_MKV, SEG_ROPE = 1920, 2176, 2304
IN_EXT = 2432
MLA_PAD = MLA_HEADS * LANE


def _params(*sem):
    return pltpu.CompilerParams(dimension_semantics=sem, vmem_limit_bytes=VMEM_LIMIT_BYTES)


def _rms(x, g):
    return x * lax.rsqrt(jnp.mean(x * x, axis=-1, keepdims=True) + EPS) * g


def _cast_kernel(w_ref, o_ref):
    o_ref[...] = w_ref[...].astype(BF16)


def _cast_stack(w):
    layers, rows, cols = w.shape
    blk = pl.BlockSpec((1, rows // CAST_BLOCKS, cols), lambda l, r: (l, r, 0))
    return pl.pallas_call(
        _cast_kernel,
        name="cast_bf16",
        grid=(layers, CAST_BLOCKS),
        in_specs=[blk],
        out_specs=blk,
        out_shape=jax.ShapeDtypeStruct(w.shape, BF16),
        compiler_params=_params("parallel", "parallel"),
    )(w)


def _stage_weight(w_hbm, layer, w_scr, stage, sem):
    rows = w_scr.shape[0]
    chunk = stage.shape[1]
    n = rows // chunk

    def copy(c):
        return pltpu.make_async_copy(w_hbm.at[layer, pl.ds(c * chunk, chunk)], stage.at[c % 2], sem.at[c % 2])

    copy(0).start()
    for c in range(n):
        if c + 1 < n:
            copy(c + 1).start()
        copy(c).wait()
        w_scr[c * chunk:(c + 1) * chunk, :] = stage[c % 2].astype(BF16)


def _ffn_kernel(*refs, fused_out, layer):
    if fused_out:
        x_ref, a_ref, b_ref, c_ref, wo_ref, g_ref, wg_hbm, wu_hbm, wd_hbm, o_ref = refs[:10]
        mix_refs = (a_ref, b_ref, c_ref)
    else:
        x_ref, g_ref, wg_hbm, wu_hbm, wd_hbm, o_ref = refs[:6]
        mix_refs = ()
    wg_ref, wu_ref, wd_ref, stage_in, stage_out, sem = refs[-6:]

    @pl.when(pl.program_id(0) == 0)
    def _():
        _stage_weight(wg_hbm, layer, wg_ref, stage_in, sem)
        _stage_weight(wu_hbm, layer, wu_ref, stage_in, sem)
        _stage_weight(wd_hbm, layer, wd_ref, stage_out, sem)

    x = x_ref[...]
    row = 0
    for part in mix_refs:
        width = part.shape[1]
        x += jnp.dot(part[...], wo_ref[row:row + width, :], preferred_element_type=F32)
        row += width
    n = _rms(x, g_ref[...]).astype(BF16)
    gate = jnp.dot(n, wg_ref[...], preferred_element_type=F32)
    up = jnp.dot(n, wu_ref[...], preferred_element_type=F32)
    h = (gate * jax.nn.sigmoid(gate) * up).astype(BF16)
    o_ref[...] = x + 0.5 * jnp.dot(h, wd_ref[...], preferred_element_type=F32)


def _ffn(x2d, layer, g, wg, wu, wd, mix=None):
    t = x2d.shape[0]
    tm = FFN_TM
    row = lambda i: (i, 0)

    def resident(a):
        return pl.BlockSpec((None,) + a.shape[1:], lambda i: (layer, 0, 0), pipeline_mode=pl.Buffered(1))

    args, in_specs = [x2d], [pl.BlockSpec((tm, D_MODEL), row)]
    if mix is not None:
        a, b, c, wo = mix
        args += [a, b, c, wo]
        in_specs += [pl.BlockSpec((tm, m.shape[1]), row) for m in (a, b, c)] + [resident(wo)]
    g3 = g[:, None, :]
    hbm = pl.BlockSpec(memory_space=pl.ANY)
    return pl.pallas_call(
        functools.partial(_ffn_kernel, fused_out=mix is not None, layer=layer),
        name="ffn_out" if mix is not None else "ffn",
        grid=(t // tm,),
        in_specs=in_specs + [resident(g3), hbm, hbm, hbm],
        out_specs=pl.BlockSpec((tm, D_MODEL), row),
        out_shape=jax.ShapeDtypeStruct((t, D_MODEL), F32),
        scratch_shapes=[
            pltpu.VMEM((D_MODEL, D_FF), BF16), pltpu.VMEM((D_MODEL, D_FF), BF16), pltpu.VMEM((D_FF, D_MODEL), BF16),
            pltpu.VMEM((2, D_MODEL // STAGE_CHUNKS, D_FF), F32), pltpu.VMEM((2, D_FF // STAGE_CHUNKS, D_MODEL), F32),
            pltpu.SemaphoreType.DMA((2,)),
        ],
        compiler_params=_params("arbitrary"),
    )(*args, g3, wg, wu, wd)


def _group_norm(x, gsum, inv_n, gain):
    ss = jnp.dot((x * x).astype(BF16), gsum, preferred_element_type=F32)
    return x * lax.rsqrt(ss * inv_n + EPS) * gain


def _group_norm_chunks(x, gsum, inv_n, gain):
    w = gsum.shape[0]
    parts = [_group_norm(x[:, c:c + w], gsum, inv_n, gain[:, c:c + w]) for c in range(0, x.shape[1], w)]
    return jnp.concatenate(parts, axis=1)


def _rope_blocks(x, cos, sin):
    lane = lax.broadcasted_iota(jnp.int32, (x.shape[0], LANE), 1)
    first_half = lane < MLA_NOPE_DIM + MLA_ROPE_DIM // 2
    parts = []
    for c in range(0, x.shape[1], LANE):
        xb = x[:, c:c + LANE]
        partner = jnp.where(first_half,
                            pltpu.roll(xb, LANE - MLA_ROPE_DIM // 2, 1),
                            pltpu.roll(xb, MLA_ROPE_DIM // 2, 1))
        parts.append(xb * cos + partner * sin)
    return jnp.concatenate(parts, axis=1)


def _proj_kernel(x_ref, g_ref, win_ref, qup_ref, kvk_ref, kvv_ref,
                 gdq_ref, gdk_ref, glq_ref, glk_ref, gmq_ref, gmkv_ref, gqn_ref, gkn_ref,
                 cos_ref, sin_ref, g32_ref, g64_ref, g128_ref,
                 dq_ref, dk_ref, dvt_ref, lq_ref, lk_ref, lvt_ref, mq_ref, mk_ref, mvt_ref):
    g32, g64, g128 = g32_ref[...], g64_ref[...], g128_ref[...]
    sub = PROJ_TM // PROJ_SPLIT
    for r in range(PROJ_SPLIT):
        rows = slice(r * sub, (r + 1) * sub)
        h = _rms(x_ref[rows, :], g_ref[...]).astype(BF16)
        proj = jnp.dot(h, win_ref[...], preferred_element_type=F32)

        def seg(lo, hi):
            return proj[:, lo:hi]

        dq_ref[rows, :] = _group_norm(seg(SEG_DQ, SEG_DK), g32, 1.0 / DIFF_QK_DIM, gdq_ref[...]).astype(BF16)
        dk_ref[rows, :] = _group_norm(seg(SEG_DK, SEG_DV), g32, 1.0 / DIFF_QK_DIM, gdk_ref[...]).astype(BF16)
        dvt_ref[0, :, rows] = seg(SEG_DV, SEG_LQ).T.astype(BF16)
        lqk = _group_norm_chunks(seg(SEG_LQ, SEG_LV), g64, 1.0 / DIL_HEAD_DIM,
                                 jnp.concatenate([glq_ref[...], glk_ref[...]], axis=1))
        lq_ref[rows, :] = lqk[:, :DIL_WIDTH].astype(BF16)
        lk_ref[rows, :] = lqk[:, DIL_WIDTH:].astype(BF16)
        lvt_ref[0, :, rows] = seg(SEG_LV, SEG_MQ).T.astype(BF16)

        cos, sin = cos_ref[rows, :], sin_ref[rows, :]
        q_lat = _rms(seg(SEG_MQ, SEG_MKV), gmq_ref[...]).astype(BF16)
        q = jnp.dot(q_lat, qup_ref[...], preferred_element_type=F32)
        q = _group_norm_chunks(q, g128, 1.0 / MLA_QK_DIM, gqn_ref[...])
        mq_ref[rows, :] = _rope_blocks(q, cos, sin).astype(BF16)

        c_kv = _rms(seg(SEG_MKV, SEG_ROPE), gmkv_ref[...]).astype(BF16)
        k_rope = seg(SEG_ROPE, IN_EXT)
        k = jnp.dot(c_kv, kvk_ref[...], preferred_element_type=F32)
        k = k + jnp.concatenate([k_rope] * MLA_HEADS, axis=1)
        k = _group_norm_chunks(k, g128, 1.0 / MLA_QK_DIM, gkn_ref[...])
        mk_ref[rows, :] = _rope_blocks(k, cos, sin).astype(BF16)
        mvt_ref[0, :, rows] = jnp.dot(c_kv, kvv_ref[...], preferred_element_type=F32).T.astype(BF16)


def _block_diag_ones(width, group):
    idx = np.arange(width) // group
    return jnp.asarray((idx[:, None] == idx[None, :]).astype(np.float32), dtype=BF16)


def _proj(x2d, pw, batch):
    t = x2d.shape[0]
    tm = PROJ_TM
    row = lambda i: (i, 0)
    fixed = lambda i: (0, 0)
    pos_blocks = SEQ // tm
    col = lambda i: (i // pos_blocks, 0, i % pos_blocks)

    def full(a):
        return pl.BlockSpec(a.shape, fixed)

    consts = [pw["g"], pw["w_in"], pw["q_up"], pw["kv_k"], pw["kv_v"],
              pw["gdq"], pw["gdk"], pw["glq"], pw["glk"], pw["gmq"], pw["gmkv"], pw["gqn"], pw["gkn"]]
    tables = [pw["cos"], pw["sin"]]
    gsums = [_block_diag_ones(256, DIFF_QK_DIM), _block_diag_ones(256, DIL_HEAD_DIM),
             _block_diag_ones(256, LANE)]
    outs = [(256, False), (256, False), (DIFF_WIDTH, True), (DIL_WIDTH, False), (DIL_WIDTH, False),
            (DIL_WIDTH, True), (MLA_PAD, False), (MLA_PAD, False), (MLA_WIDTH, True)]
    return pl.pallas_call(
        _proj_kernel,
        name="proj",
        grid=(t // tm,),
        in_specs=([pl.BlockSpec((tm, D_MODEL), row)] + [full(a) for a in consts]
                  + [pl.BlockSpec((tm, LANE), lambda i: (i % pos_blocks, 0)) for _ in tables]
                  + [full(a) for a in gsums]),
        out_specs=[pl.BlockSpec((1, w, tm), col) if tr else pl.BlockSpec((tm, w), row) for w, tr in outs],
        out_shape=[jax.ShapeDtypeStruct((batch, w, SEQ) if tr else (t, w), BF16) for w, tr in outs],
        compiler_params=_params("parallel"),
    )(x2d, *consts, *tables, *gsums)


def _strip_kernel(rev_ref, o_ref, src_scr):
    row = rev_ref[0]
    for s in range(SUBLANE):
        shift = SUBLANE - 1 - s
        src_scr[s:s + 1, :] = pltpu.roll(row, STRIP_SRC_W - shift, 1) if shift else row
    chunk_w = ATT_TQ + LANE
    for a in range(STRIP_ROWS // SUBLANE):
        start = STRIP_ROWS - SUBLANE - SUBLANE * a
        lo = (start // LANE) * LANE
        chunk = src_scr[:, lo:lo + chunk_w]
        shift = start - lo
        if shift:
            chunk = pltpu.roll(chunk, chunk_w - shift, 1)
        o_ref[0, a * SUBLANE:(a + 1) * SUBLANE, :] = chunk[:, :ATT_TQ]


def _bias_strips(rev):
    nh = rev.shape[0]
    return pl.pallas_call(
        _strip_kernel,
        name="bias_strips",
        grid=(nh,),
        in_specs=[pl.BlockSpec((1, 1, STRIP_SRC_W), lambda h: (h, 0, 0))],
        out_specs=pl.BlockSpec((1, STRIP_ROWS, ATT_TQ), lambda h: (h, 0, 0)),
        out_shape=jax.ShapeDtypeStruct((nh, STRIP_ROWS, ATT_TQ), F32),
        scratch_shapes=[pltpu.VMEM((SUBLANE, STRIP_SRC_W), F32)],
        compiler_params=_params("parallel"),
    )(rev)


def _t5_bucket(rel):
    half = NUM_BUCKETS // 2
    max_exact = half // 2
    n = jnp.abs(rel)
    nf = jnp.maximum(n, 1).astype(F32)
    large = max_exact + (jnp.log(nf / max_exact) / math.log(MAX_DISTANCE / max_exact)
                         * (half - max_exact)).astype(jnp.int32)
    large = jnp.minimum(large, half - 1)
    return jnp.where(rel > 0, half, 0) + jnp.where(n < max_exact, n, large)


def _strip_sources(rel_bias):
    rel = np.arange(-(SEQ - 1), SEQ)
    mult = np.zeros(rel.shape, np.int64)
    for window, dil in DIL_PATTERNS:
        mult += (rel % dil == 0) & (np.abs(rel) // dil <= window // (2 * dil))
    bucket = _t5_bucket(jnp.asarray(rel, jnp.int32))
    table = rel_bias.astype(F32)[bucket].T
    is_dil = jnp.asarray(np.arange(BIAS_HEADS) >= DIFF_HEADS)[:, None]
    logm = jnp.asarray(np.log(np.maximum(mult, 1)), F32)[None, :]
    reachable = jnp.asarray(mult > 0)[None, :]
    vec = jnp.where(is_dil & ~reachable, NEG_BIG, table + jnp.where(is_dil, logm, 0.0)) * LOG2E
    rev = jnp.pad(vec[:, ::-1], ((0, 0), (0, STRIP_SRC_W - vec.shape[1])))
    return rev[:, None, :]


_NT = (((1,), (1,)), ((), ()))


def _lanes_between(shape, lo, hi):
    lane = lax.broadcasted_iota(jnp.int32, shape, 1)
    return (lane >= lo) & (lane < hi)


def _values_with_ones(vt, hh):
    return jnp.concatenate([vt[hh * HEAD_V:(hh + 1) * HEAD_V], jnp.ones((BF16_ROWS, vt.shape[1]), BF16)], axis=0)


def _col_max(s_t):
    part = jnp.max(s_t.reshape(s_t.shape[0] // MAX_SLAB, MAX_SLAB, s_t.shape[1]), axis=0)
    return jnp.max(part, axis=0, keepdims=True)


def _attend(n_items, logits_fn, values_fn, bounded, needed=lambda item, chunk: True):
    stages = [(i, c) for i in range(n_items) for c in range(SEQ // ATT_TK) if needed(i, c)]
    outs = []
    s_next = logits_fn(*stages[0])
    m = acc = None
    for idx, (item, c) in enumerate(stages):
        first = idx == 0 or stages[idx - 1][0] != item
        last = idx + 1 == len(stages) or stages[idx + 1][0] != item
        s_cur = s_next
        if idx + 1 < len(stages):
            s_next = logits_fn(*stages[idx + 1])
        values = values_fn(item)[:, c * ATT_TK:(c + 1) * ATT_TK]
        if bounded:
            pv = jnp.dot(values, jnp.exp2(s_cur).astype(BF16), preferred_element_type=F32)
            acc = pv if first else acc + pv
        else:
            c_max = _col_max(s_cur)
            m_new = c_max if first else jnp.maximum(m, c_max)
            pv = jnp.dot(values, jnp.exp2((s_cur - m_new).astype(BF16)), preferred_element_type=F32)
            acc = pv if first else acc * jnp.exp2(m - m_new) + pv
            m = m_new
        if last:
            outs.append(acc[:HEAD_V] * (1.0 / acc[HEAD_V:HEAD_V + 1]))
    return outs


def _q_rows(j):
    return slice(j * ATT_TQ, (j + 1) * ATT_TQ)


def _k_rows(c):
    return slice(c * ATT_TK, (c + 1) * ATT_TK)


def _strip_window(strip_ref, hh, qblk, c):
    off = (N_QBLK - 1 - qblk) * ATT_TQ + c * ATT_TK
    return strip_ref[hh, off:off + ATT_TK, :]


def _dil_chunk_needed(qblk, c):
    reach = max(window // 2 for window, _ in DIL_PATTERNS)
    q_lo, q_hi = qblk * ATT_TQ, (qblk + 1) * ATT_TQ - 1
    k_lo, k_hi = c * ATT_TK, (c + 1) * ATT_TK - 1
    return max(k_lo - q_hi, q_lo - k_hi, 0) <= reach


def _masked_q(q_ref, j, lo, hi):
    q = q_ref[0, _q_rows(j), :]
    return jnp.where(_lanes_between(q.shape, lo, hi), q, jnp.zeros_like(q))


def _diff_kernel(q_ref, k_ref, vt_ref, strip_ref, lam_ref, g_ref, o_ref, *, lambda_init, bounded):
    vt = vt_ref[0]
    lp = lam_ref[...]
    lam = (jnp.exp(jnp.sum(lp[0:1] * lp[1:2], axis=-1, keepdims=True))
           - jnp.exp(jnp.sum(lp[2:3] * lp[3:4], axis=-1, keepdims=True)) + lambda_init)
    subs = 2 * 2
    v_ext = [_values_with_ones(vt, hh) for hh in range(2)]

    def logits(i, c):
        j, sub = divmod(i, subs)
        qm = _masked_q(q_ref, j, DIFF_QK_DIM * sub, DIFF_QK_DIM * (sub + 1))
        return (lax.dot_general(k_ref[0, _k_rows(c), :], qm, _NT, preferred_element_type=F32)
                + _strip_window(strip_ref, sub // 2, j, c))

    parts = _attend(N_QBLK * subs, logits, lambda i: v_ext[(i % subs) // 2], bounded)
    for j in range(N_QBLK):
        outs = []
        for hh in range(2):
            comb = parts[j * subs + 2 * hh] - lam * parts[j * subs + 2 * hh + 1]
            ss = jnp.sum(comb * comb, axis=0, keepdims=True)
            outs.append(comb * lax.rsqrt(ss * (1.0 / DIFF_V_DIM) + EPS))
        y = jnp.concatenate(outs, axis=0).T * g_ref[...] * (1.0 - lambda_init)
        o_ref[0, _q_rows(j), :] = y.astype(BF16)


def _dil_kernel(q_ref, k_ref, vt_ref, strip_ref, o_ref, *, bounded):
    vt = vt_ref[0]
    v_ext = [_values_with_ones(vt, hh) for hh in range(2)]

    def logits(i, c):
        j, hh = divmod(i, 2)
        qm = _masked_q(q_ref, j, DIL_HEAD_DIM * hh, DIL_HEAD_DIM * (hh + 1))
        return (lax.dot_general(k_ref[0, _k_rows(c), :], qm, _NT, preferred_element_type=F32)
                + _strip_window(strip_ref, hh, j, c))

    outs = _attend(N_QBLK * 2, logits, lambda i: v_ext[i % 2], bounded,
                   needed=lambda i, c: _dil_chunk_needed(i // 2, c))
    for j in range(N_QBLK):
        o_ref[0, _q_rows(j), :] = jnp.concatenate(outs[2 * j:2 * j + 2], axis=0).T.astype(BF16)


def _mla_kernel(q_ref, k_ref, vt_ref, o_ref, *, bounded):
    vt = vt_ref[0]
    v_ext = [_values_with_ones(vt, hh) for hh in range(2)]

    def logits(i, c):
        j, hh = divmod(i, 2)
        return lax.dot_general(k_ref[0, _k_rows(c), hh * LANE:(hh + 1) * LANE],
                               q_ref[0, _q_rows(j), hh * LANE:(hh + 1) * LANE], _NT,
                               preferred_element_type=F32)

    outs = _attend(N_QBLK * 2, logits, lambda i: v_ext[i % 2], bounded)
    for j in range(N_QBLK):
        o_ref[0, _q_rows(j), :] = jnp.concatenate(outs[2 * j:2 * j + 2], axis=0).T.astype(BF16)


def _attention(kernel, name, bounded, q, k, vt, qk_lanes, strips=None, head0=0, extra=()):
    b, s, _ = q.shape
    n_pairs = vt.shape[1] // LANE
    in_specs = [
        pl.BlockSpec((1, s, qk_lanes), lambda p, bb: (bb, 0, p)),
        pl.BlockSpec((1, s, qk_lanes), lambda p, bb: (bb, 0, p)),
        pl.BlockSpec((1, LANE, s), lambda p, bb: (bb, p, 0)),
    ]
    args = [q, k, vt]
    if strips is not None:
        in_specs.append(pl.BlockSpec((2, STRIP_ROWS, ATT_TQ), lambda p, bb: (head0 // 2 + p, 0, 0),
                                     pipeline_mode=pl.Buffered(1)))
        args.append(strips)
    for a in extra:
        per_pair = a.shape[-1] == n_pairs * LANE
        in_specs.append(pl.BlockSpec((1, LANE), lambda p, bb: (0, p)) if per_pair
                        else pl.BlockSpec(a.shape, lambda p, bb: (0, 0)))
        args.append(a)

    return pl.pallas_call(
        functools.partial(kernel, bounded=bounded),
        name=name + ("_bounded" if bounded else "_online"),
        grid=(n_pairs, b),
        in_specs=in_specs,
        out_specs=pl.BlockSpec((1, s, LANE), lambda p, bb: (bb, 0, p)),
        out_shape=jax.ShapeDtypeStruct((b, s, n_pairs * LANE), BF16),
        compiler_params=_params("parallel", "parallel"),
    )(*args)


def _mixers(bounded, lambda_init):
    def run(dq, dk, dvt, lq, lk, lvt, mq, mk, mvt, strips, lam, subln):
        out_a = _attention(functools.partial(_diff_kernel, lambda_init=lambda_init), "diff_attn", bounded,
                           dq, dk, dvt, LANE, strips, 0, extra=(lam, subln))
        out_b = _attention(_dil_kernel, "dil_attn", bounded, lq, lk, lvt, LANE, strips, DIFF_HEADS)
        out_c = _attention(_mla_kernel, "mla_attn", bounded, mq, mk, mvt, 2 * LANE)
        return out_a, out_b, out_c
    return run


def _logit_bound(gain_q, gain_k, dim, bias_abs=0.0):
    return dim * jnp.max(jnp.abs(gain_q)) * jnp.max(jnp.abs(gain_k)) + bias_abs


def _rope_tables():
    half = MLA_ROPE_DIM // 2
    inv = ROPE_THETA ** (-jnp.arange(half, dtype=F32) / half)
    ang = jnp.arange(SEQ).astype(F32)[:, None] * inv[None, :]
    cos, sin = jnp.cos(ang), jnp.sin(ang)
    ones = jnp.ones((SEQ, MLA_NOPE_DIM), F32)
    tail = LANE - MLA_QK_DIM
    cos_t = jnp.concatenate([ones, cos, cos, jnp.ones((SEQ, tail), F32)], axis=1)
    sin_t = jnp.concatenate([0.0 * ones, -sin, sin, jnp.zeros((SEQ, tail), F32)], axis=1)
    return cos_t, sin_t


def _proj_weights(l, mix_norm, w_in, diff_q_norm, diff_k_norm, dil_q_norm, dil_k_norm,
                  mla_q_norm, mla_q_up, mla_kv_norm, mla_kv_up, mla_qn, mla_kn, cos_t, sin_t):
    wi = w_in[l]
    pad_to_block = LANE - MLA_QK_DIM
    w_ext = jnp.concatenate([wi[:, :SEG_ROPE], jnp.zeros((D_MODEL, MLA_NOPE_DIM), F32),
                             wi[:, SEG_ROPE:], jnp.zeros((D_MODEL, pad_to_block), F32)], axis=1)
    q_up = jnp.pad(mla_q_up[l].reshape(MLA_Q_RANK, MLA_HEADS, MLA_QK_DIM), ((0, 0), (0, 0), (0, pad_to_block)))
    kv = mla_kv_up[l].reshape(MLA_KV_RANK, MLA_HEADS, MLA_NOPE_DIM + MLA_V_DIM)
    kv_k = jnp.pad(kv[:, :, :MLA_NOPE_DIM], ((0, 0), (0, 0), (0, LANE - MLA_NOPE_DIM)))
    kv_v = kv[:, :, MLA_NOPE_DIM:]

    def head_gain(g, heads, scale=1.0, pad=0):
        return jnp.tile(jnp.pad(g.astype(F32) * scale, (0, pad)), heads)[None, :]

    return {
        "g": mix_norm[l][None, :],
        "w_in": w_ext.astype(BF16),
        "q_up": q_up.reshape(MLA_Q_RANK, MLA_PAD).astype(BF16),
        "kv_k": kv_k.reshape(MLA_KV_RANK, MLA_PAD).astype(BF16),
        "kv_v": kv_v.reshape(MLA_KV_RANK, MLA_WIDTH).astype(BF16),
        "gdq": head_gain(diff_q_norm[l], 2 * DIFF_HEADS, DIFF_QK_DIM ** -0.5 * LOG2E),
        "gdk": head_gain(diff_k_norm[l], 2 * DIFF_HEADS),
        "glq": head_gain(dil_q_norm[l], DIL_HEADS, DIL_HEAD_DIM ** -0.5 * LOG2E),
        "glk": head_gain(dil_k_norm[l], DIL_HEADS),
        "gmq": mla_q_norm[l][None, :],
        "gmkv": mla_kv_norm[l][None, :],
        "gqn": head_gain(mla_qn[l], MLA_HEADS, MLA_QK_DIM ** -0.5 * LOG2E, pad_to_block),
        "gkn": head_gain(mla_kn[l], MLA_HEADS, 1.0, pad_to_block),
        "cos": cos_t,
        "sin": sin_t,
    }


def kernel(x, rel_bias, ffn1_norm, ffn1_wg, ffn1_wu, ffn1_wd, mix_norm, w_in, diff_q_norm, diff_k_norm, diff_lambda, diff_subln, dil_q_norm, dil_k_norm, mla_q_norm, mla_q_up, mla_kv_norm, mla_kv_up, mla_qn, mla_kn, w_o, ffn2_norm, ffn2_wg, ffn2_wu, ffn2_wd):
    b, s, d = x.shape
    assert (s, d) == (SEQ, D_MODEL)
    t = b * s
    strips = _bias_strips(_strip_sources(rel_bias))
    cos_t, sin_t = _rope_tables()
    x2d = x.reshape(t, d)
    ffn1 = (ffn1_wg, ffn1_wu, ffn1_wd)
    ffn2 = (ffn2_wg, ffn2_wu, ffn2_wd)
    w_o_bf16 = _cast_stack(w_o)
    for l in range(DEPTH):
        lambda_init = 0.8 - 0.6 * math.exp(-0.3 * (l + 1))
        x2d = _ffn(x2d, l, ffn1_norm, *ffn1)
        pw = _proj_weights(l, mix_norm, w_in, diff_q_norm, diff_k_norm, dil_q_norm, dil_k_norm,
                           mla_q_norm, mla_q_up, mla_kv_norm, mla_kv_up, mla_qn, mla_kn, cos_t, sin_t)
        dq, dk, dvt, lq, lk, lvt, mq, mk, mvt = _proj(x2d, pw, b)
        tok = lambda a: a.reshape(b, s, -1)
        subln = jnp.tile(diff_subln[l].astype(F32), DIFF_HEADS)[None, :]
        bias_abs = jnp.max(jnp.abs(rel_bias.astype(F32)), axis=0) * LOG2E
        bound_a = _logit_bound(pw["gdq"], pw["gdk"], DIFF_QK_DIM, jnp.max(bias_abs[:DIFF_HEADS]))
        bound_b = _logit_bound(pw["glq"], pw["glk"], DIL_HEAD_DIM,
                               jnp.max(bias_abs[DIFF_HEADS:]) + math.log2(len(DIL_PATTERNS)))
        bound_c = _logit_bound(pw["gqn"], pw["gkn"], MLA_QK_DIM)
        logits_bounded = jnp.maximum(jnp.maximum(bound_a, bound_b), bound_c) <= LOGIT_BOUND
        out_a, out_b, out_c = lax.cond(
            logits_bounded, _mixers(True, lambda_init), _mixers(False, lambda_init),
            tok(dq), tok(dk), dvt, tok(lq), tok(lk), lvt, tok(mq), tok(mk), mvt, strips,
            diff_lambda[l].astype(F32), subln)
        mix = (out_a.reshape(t, -1), out_b.reshape(t, -1), out_c.reshape(t, -1), w_o_bf16)
        x2d = _ffn(x2d, l, ffn2_norm, *ffn2, mix=mix)
    return x2d.reshape(b, s, d)
```

```python
import functools
import math

import jax
import jax.numpy as jnp
import numpy as np
from jax import lax
from jax.experimental import pallas as pl
from jax.experimental.pallas import tpu as pltpu

F32 = jnp.float32
BF16 = jnp.bfloat16

D_MODEL = 1024
SEQ = 2048
DEPTH = 2
EPS = 1e-6
D_FF = 2816
NUM_BUCKETS = 32
MAX_DISTANCE = 1024
ROPE_THETA = 10000.0
NEG_BIG = -1e30
LOG2E = math.log2(math.e)
DIFF_HEADS, DIFF_QK_DIM, DIFF_V_DIM = 4, 32, 64
DIL_HEADS, DIL_HEAD_DIM = 6, 64
DIL_PATTERNS = ((128, 1), (512, 4), (2048, 16))
MLA_HEADS, MLA_Q_RANK, MLA_KV_RANK = 6, 256, 128
MLA_NOPE_DIM, MLA_ROPE_DIM, MLA_V_DIM = 64, 32, 64
MLA_QK_DIM = MLA_NOPE_DIM + MLA_ROPE_DIM
HEAD_V = 64
DIFF_WIDTH = DIFF_HEADS * DIFF_V_DIM
DIL_WIDTH = DIL_HEADS * DIL_HEAD_DIM
MLA_WIDTH = MLA_HEADS * MLA_V_DIM
BIAS_HEADS = DIFF_HEADS + DIL_HEADS

LANE = 128
SUBLANE = 8
BF16_ROWS = 16
MXU_DIM = 256
VMEM_LIMIT_BYTES = 56 * 1024 * 1024

FFN_TM = 512
CAST_BLOCKS = 4
STAGE_CHUNKS = 8
PROJ_TM = 512
PROJ_SPLIT = 2
ATT_TQ = 512
ATT_TK = 1024
DIL_TK = 512
MAX_SLAB = 128
LOGIT_BOUND = 50.0
STRIP_ROWS = 2 * SEQ - ATT_TQ
STRIP_SRC_W = 2 * SEQ + 2 * LANE
N_QBLK = SEQ // ATT_TQ

DIFF_QK_WIDTH = DIFF_HEADS * 2 * DIFF_QK_DIM
SEG_DQ = 0
SEG_DK = SEG_DQ + DIFF_QK_WIDTH
SEG_DV = SEG_DK + DIFF_QK_WIDTH
SEG_LQ = SEG_DV + DIFF_WIDTH
SEG_LK = SEG_LQ + DIL_WIDTH
SEG_LV = SEG_LK + DIL_WIDTH
SEG_MQ = SEG_LV + DIL_WIDTH
SEG_MKV = SEG_MQ + MLA_Q_RANK
SEG_ROPE = SEG_MKV + MLA_KV_RANK
IN_EXT = SEG_ROPE + LANE
MLA_PAD = MLA_HEADS * LANE


def _params(*sem):
    return pltpu.CompilerParams(dimension_semantics=sem, vmem_limit_bytes=VMEM_LIMIT_BYTES)


def _rms(x, g):
    return x * lax.rsqrt(jnp.mean(x * x, axis=-1, keepdims=True) + EPS) * g


def _cast_kernel(w_ref, o_ref):
    o_ref[...] = w_ref[...].astype(BF16)


def _cast_stack(w):
    layers, rows, cols = w.shape
    blk = pl.BlockSpec((1, rows // CAST_BLOCKS, cols), lambda l, r: (l, r, 0))
    return pl.pallas_call(
        _cast_kernel,
        name="cast_bf16",
        grid=(layers, CAST_BLOCKS),
        in_specs=[blk],
        out_specs=blk,
        out_shape=jax.ShapeDtypeStruct(w.shape, BF16),
        compiler_params=_params("parallel", "parallel"),
    )(w)


def _stage_weight(w_hbm, layer, w_scr, stage, sem):
    rows = w_scr.shape[0]
    chunk = stage.shape[1]
    n = rows // chunk

    def copy(c):
        return pltpu.make_async_copy(w_hbm.at[layer, pl.ds(c * chunk, chunk)], stage.at[c % 2], sem.at[c % 2])

    copy(0).start()
    for c in range(n):
        if c + 1 < n:
            copy(c + 1).start()
        copy(c).wait()
        w_scr[c * chunk:(c + 1) * chunk, :] = stage[c % 2].astype(BF16)


def _ffn_kernel(*refs, fused_out, layer):
    if fused_out:
        x_ref, a_ref, b_ref, c_ref, wo_ref, g_ref, wg_hbm, wu_hbm, wd_hbm, o_ref = refs[:10]
        mix_refs = (a_ref, b_ref, c_ref)
    else:
        x_ref, g_ref, wg_hbm, wu_hbm, wd_hbm, o_ref = refs[:6]
        mix_refs = ()
    wg_ref, wu_ref, wd_ref, stage_in, stage_out, sem = refs[-6:]

    @pl.when(pl.program_id(0) == 0)
    def _():
        _stage_weight(wg_hbm, layer, wg_ref, stage_in, sem)
        _stage_weight(wu_hbm, layer, wu_ref, stage_in, sem)
        _stage_weight(wd_hbm, layer, wd_ref, stage_out, sem)

    x = x_ref[...]
    row = 0
    for part in mix_refs:
        width = part.shape[1]
        x += jnp.dot(part[...], wo_ref[row:row + width, :], preferred_element_type=F32)
        row += width
    n = _rms(x, g_ref[...]).astype(BF16)
    gate = jnp.dot(n, wg_ref[...], preferred_element_type=F32)
    up = jnp.dot(n, wu_ref[...], preferred_element_type=F32)
    h = (gate * jax.nn.sigmoid(gate) * up).astype(BF16)
    o_ref[...] = x + 0.5 * jnp.dot(h, wd_ref[...], preferred_element_type=F32)


def _ffn(x2d, layer, g, wg, wu, wd, mix=None):
    t = x2d.shape[0]
    tm = FFN_TM
    row = lambda i: (i, 0)

    def resident(a):
        return pl.BlockSpec((None,) + a.shape[1:], lambda i: (layer, 0, 0), pipeline_mode=pl.Buffered(1))

    args, in_specs = [x2d], [pl.BlockSpec((tm, D_MODEL), row)]
    if mix is not None:
        a, b, c, wo = mix
        args += [a, b, c, wo]
        in_specs += [pl.BlockSpec((tm, m.shape[1]), row) for m in (a, b, c)] + [resident(wo)]
    g3 = g[:, None, :]
    hbm = pl.BlockSpec(memory_space=pl.ANY)
    return pl.pallas_call(
        functools.partial(_ffn_kernel, fused_out=mix is not None, layer=layer),
        name="ffn_out" if mix is not None else "ffn",
        grid=(t // tm,),
        in_specs=in_specs + [resident(g3), hbm, hbm, hbm],
        out_specs=pl.BlockSpec((tm, D_MODEL), row),
        out_shape=jax.ShapeDtypeStruct((t, D_MODEL), F32),
        scratch_shapes=[
            pltpu.VMEM((D_MODEL, D_FF), BF16), pltpu.VMEM((D_MODEL, D_FF), BF16), pltpu.VMEM((D_FF, D_MODEL), BF16),
            pltpu.VMEM((2, D_MODEL // STAGE_CHUNKS, D_FF), F32), pltpu.VMEM((2, D_FF // STAGE_CHUNKS, D_MODEL), F32),
            pltpu.SemaphoreType.DMA((2,)),
        ],
        compiler_params=_params("arbitrary"),
    )(*args, g3, wg, wu, wd)


def _group_norm(x, gsum, inv_n, gain):
    ss = jnp.dot((x * x).astype(BF16), gsum, preferred_element_type=F32)
    return x * lax.rsqrt(ss * inv_n + EPS) * gain


def _group_norm_chunks(x, gsum, inv_n, gain):
    w = gsum.shape[0]
    parts = [_group_norm(x[:, c:c + w], gsum, inv_n, gain[:, c:c + w]) for c in range(0, x.shape[1], w)]
    return jnp.concatenate(parts, axis=1)


def _rope_blocks(x, cos, sin):
    lane = lax.broadcasted_iota(jnp.int32, (x.shape[0], LANE), 1)
    first_half = lane < MLA_NOPE_DIM + MLA_ROPE_DIM // 2
    parts = []
    for c in range(0, x.shape[1], LANE):
        xb = x[:, c:c + LANE]
        partner = jnp.where(first_half,
                            pltpu.roll(xb, LANE - MLA_ROPE_DIM // 2, 1),
                            pltpu.roll(xb, MLA_ROPE_DIM // 2, 1))
        parts.append(xb * cos + partner * sin)
    return jnp.concatenate(parts, axis=1)


def _proj_kernel(x_ref, g_ref, win_ref, qup_ref, kvk_ref, kvv_ref,
                 gdq_ref, gdk_ref, glq_ref, glk_ref, gmq_ref, gmkv_ref, gqn_ref, gkn_ref,
                 cos_ref, sin_ref, g32_ref, g64_ref, g128_ref,
                 dq_ref, dk_ref, dvt_ref, lq_ref, lk_ref, lvt_ref, mq_ref, mk_ref, mvt_ref):
    g32, g64, g128 = g32_ref[...], g64_ref[...], g128_ref[...]
    sub = PROJ_TM // PROJ_SPLIT
    for r in range(PROJ_SPLIT):
        rows = slice(r * sub, (r + 1) * sub)
        h = _rms(x_ref[rows, :], g_ref[...]).astype(BF16)
        proj = jnp.dot(h, win_ref[...], preferred_element_type=F32)

        def seg(lo, hi):
            return proj[:, lo:hi]

        dq_ref[rows, :] = _group_norm(seg(SEG_DQ, SEG_DK), g32, 1.0 / DIFF_QK_DIM, gdq_ref[...]).astype(BF16)
        dk_ref[rows, :] = _group_norm(seg(SEG_DK, SEG_DV), g32, 1.0 / DIFF_QK_DIM, gdk_ref[...]).astype(BF16)
        dvt_ref[0, :, rows] = seg(SEG_DV, SEG_LQ).T.astype(BF16)
        lqk = _group_norm_chunks(seg(SEG_LQ, SEG_LV), g64, 1.0 / DIL_HEAD_DIM,
                                 jnp.concatenate([glq_ref[...], glk_ref[...]], axis=1))
        lq_ref[rows, :] = lqk[:, :DIL_WIDTH].astype(BF16)
        lk_ref[rows, :] = lqk[:, DIL_WIDTH:].astype(BF16)
        lvt_ref[0, :, rows] = seg(SEG_LV, SEG_MQ).T.astype(BF16)

        cos, sin = cos_ref[rows, :], sin_ref[rows, :]
        q_lat = _rms(seg(SEG_MQ, SEG_MKV), gmq_ref[...]).astype(BF16)
        q = jnp.dot(q_lat, qup_ref[...], preferred_element_type=F32)
        q = _group_norm_chunks(q, g128, 1.0 / MLA_QK_DIM, gqn_ref[...])
        mq_ref[rows, :] = _rope_blocks(q, cos, sin).astype(BF16)

        c_kv = _rms(seg(SEG_MKV, SEG_ROPE), gmkv_ref[...]).astype(BF16)
        k_rope = seg(SEG_ROPE, IN_EXT)
        k = jnp.dot(c_kv, kvk_ref[...], preferred_element_type=F32)
        k = k + jnp.concatenate([k_rope] * MLA_HEADS, axis=1)
        k = _group_norm_chunks(k, g128, 1.0 / MLA_QK_DIM, gkn_ref[...])
        mk_ref[rows, :] = _rope_blocks(k, cos, sin).astype(BF16)
        mvt_ref[0, :, rows] = jnp.dot(c_kv, kvv_ref[...], preferred_element_type=F32).T.astype(BF16)


def _block_diag_ones(width, group):
    idx = np.arange(width) // group
    return jnp.asarray((idx[:, None] == idx[None, :]).astype(np.float32), dtype=BF16)


def _proj(x2d, pw, batch):
    t = x2d.shape[0]
    tm = PROJ_TM
    row = lambda i: (i, 0)
    fixed = lambda i: (0, 0)
    pos_blocks = SEQ // tm
    col = lambda i: (i // pos_blocks, 0, i % pos_blocks)

    def full(a):
        return pl.BlockSpec(a.shape, fixed)

    consts = [pw["g"], pw["w_in"], pw["q_up"], pw["kv_k"], pw["kv_v"],
              pw["gdq"], pw["gdk"], pw["glq"], pw["glk"], pw["gmq"], pw["gmkv"], pw["gqn"], pw["gkn"]]
    tables = [pw["cos"], pw["sin"]]
    gsums = [_block_diag_ones(MXU_DIM, DIFF_QK_DIM), _block_diag_ones(MXU_DIM, DIL_HEAD_DIM),
             _block_diag_ones(MXU_DIM, LANE)]
    outs = [(DIFF_QK_WIDTH, False), (DIFF_QK_WIDTH, False), (DIFF_WIDTH, True), (DIL_WIDTH, False), (DIL_WIDTH, False),
            (DIL_WIDTH, True), (MLA_PAD, False), (MLA_PAD, False), (MLA_WIDTH, True)]
    return pl.pallas_call(
        _proj_kernel,
        name="proj",
        grid=(t // tm,),
        in_specs=([pl.BlockSpec((tm, D_MODEL), row)] + [full(a) for a in consts]
                  + [pl.BlockSpec((tm, LANE), lambda i: (i % pos_blocks, 0)) for _ in tables]
                  + [full(a) for a in gsums]),
        out_specs=[pl.BlockSpec((1, w, tm), col) if tr else pl.BlockSpec((tm, w), row) for w, tr in outs],
        out_shape=[jax.ShapeDtypeStruct((batch, w, SEQ) if tr else (t, w), BF16) for w, tr in outs],
        compiler_params=_params("parallel"),
    )(x2d, *consts, *tables, *gsums)


def _strip_kernel(rev_ref, o_ref, src_scr):
    row = rev_ref[0]
    for s in range(SUBLANE):
        shift = SUBLANE - 1 - s
        src_scr[s:s + 1, :] = pltpu.roll(row, STRIP_SRC_W - shift, 1) if shift else row
    chunk_w = ATT_TQ + LANE
    for a in range(STRIP_ROWS // SUBLANE):
        start = STRIP_ROWS - SUBLANE - SUBLANE * a
        lo = (start // LANE) * LANE
        chunk = src_scr[:, lo:lo + chunk_w]
        shift = start - lo
        if shift:
            chunk = pltpu.roll(chunk, chunk_w - shift, 1)
        o_ref[0, a * SUBLANE:(a + 1) * SUBLANE, :] = chunk[:, :ATT_TQ]


def _bias_strips(rev):
    nh = rev.shape[0]
    return pl.pallas_call(
        _strip_kernel,
        name="bias_strips",
        grid=(nh,),
        in_specs=[pl.BlockSpec((1, 1, STRIP_SRC_W), lambda h: (h, 0, 0))],
        out_specs=pl.BlockSpec((1, STRIP_ROWS, ATT_TQ), lambda h: (h, 0, 0)),
        out_shape=jax.ShapeDtypeStruct((nh, STRIP_ROWS, ATT_TQ), F32),
        scratch_shapes=[pltpu.VMEM((SUBLANE, STRIP_SRC_W), F32)],
        compiler_params=_params("parallel"),
    )(rev)


def _t5_bucket(rel):
    half = NUM_BUCKETS // 2
    max_exact = half // 2
    n = jnp.abs(rel)
    nf = jnp.maximum(n, 1).astype(F32)
    large = max_exact + (jnp.log(nf / max_exact) / math.log(MAX_DISTANCE / max_exact)
                         * (half - max_exact)).astype(jnp.int32)
    large = jnp.minimum(large, half - 1)
    return jnp.where(rel > 0, half, 0) + jnp.where(n < max_exact, n, large)


def _strip_sources(rel_bias):
    rel = np.arange(-(SEQ - 1), SEQ)
    mult = np.zeros(rel.shape, np.int64)
    for window, dil in DIL_PATTERNS:
        mult += (rel % dil == 0) & (np.abs(rel) // dil <= window // (2 * dil))
    bucket = _t5_bucket(jnp.asarray(rel, jnp.int32))
    table = rel_bias.astype(F32)[bucket].T
    is_dil = jnp.asarray(np.arange(BIAS_HEADS) >= DIFF_HEADS)[:, None]
    logm = jnp.asarray(np.log(np.maximum(mult, 1)), F32)[None, :]
    reachable = jnp.asarray(mult > 0)[None, :]
    vec = jnp.where(is_dil & ~reachable, NEG_BIG, table + jnp.where(is_dil, logm, 0.0)) * LOG2E
    rev = jnp.pad(vec[:, ::-1], ((0, 0), (0, STRIP_SRC_W - vec.shape[1])))
    return rev[:, None, :]


_NT = (((1,), (1,)), ((), ()))


def _lanes_between(shape, lo, hi):
    lane = lax.broadcasted_iota(jnp.int32, shape, 1)
    return (lane >= lo) & (lane < hi)


def _values_with_ones(vt, hh):
    return jnp.concatenate([vt[hh * HEAD_V:(hh + 1) * HEAD_V], jnp.ones((BF16_ROWS, vt.shape[1]), BF16)], axis=0)


def _col_max(s_t):
    part = jnp.max(s_t.reshape(s_t.shape[0] // MAX_SLAB, MAX_SLAB, s_t.shape[1]), axis=0)
    return jnp.max(part, axis=0, keepdims=True)


def _attend(n_items, logits_fn, values_fn, bounded, tk=ATT_TK, needed=lambda item, chunk: True):
    stages = [(i, c) for i in range(n_items) for c in range(SEQ // tk) if needed(i, c)]
    outs = []
    s_next = logits_fn(*stages[0])
    m = acc = None
    for idx, (item, c) in enumerate(stages):
        first = idx == 0 or stages[idx - 1][0] != item
        last = idx + 1 == len(stages) or stages[idx + 1][0] != item
        s_cur = s_next
        if idx + 1 < len(stages):
            s_next = logits_fn(*stages[idx + 1])
        values = values_fn(item)[:, c * tk:(c + 1) * tk]
        if bounded:
            pv = jnp.dot(values, jnp.exp2(s_cur).astype(BF16), preferred_element_type=F32)
            acc = pv if first else acc + pv
        else:
            c_max = _col_max(s_cur)
            m_new = c_max if first else jnp.maximum(m, c_max)
            pv = jnp.dot(values, jnp.exp2((s_cur - m_new).astype(BF16)), preferred_element_type=F32)
            acc = pv if first else acc * jnp.exp2(m - m_new) + pv
            m = m_new
        if last:
            outs.append(acc[:HEAD_V] * (1.0 / acc[HEAD_V:HEAD_V + 1]))
    return outs


def _q_rows(j):
    return slice(j * ATT_TQ, (j + 1) * ATT_TQ)


def _k_rows(c, tk=ATT_TK):
    return slice(c * tk, (c + 1) * tk)


def _strip_window(strip_ref, hh, qblk, c, tk=ATT_TK):
    off = (N_QBLK - 1 - qblk) * ATT_TQ + c * tk
    return strip_ref[hh, off:off + tk, :]


def _dil_chunk_needed(qblk, c):
    reach = max(window // 2 for window, _ in DIL_PATTERNS)
    q_lo, q_hi = qblk * ATT_TQ, (qblk + 1) * ATT_TQ - 1
    k_lo, k_hi = c * DIL_TK, (c + 1) * DIL_TK - 1
    return max(k_lo - q_hi, q_lo - k_hi, 0) <= reach


def _masked_q(q_ref, j, lo, hi):
    q = q_ref[0, _q_rows(j), :]
    return jnp.where(_lanes_between(q.shape, lo, hi), q, jnp.zeros_like(q))


def _diff_kernel(q_ref, k_ref, vt_ref, strip_ref, lam_ref, g_ref, o_ref, *, lambda_init, bounded):
    vt = vt_ref[0]
    lp = lam_ref[...]
    lam = (jnp.exp(jnp.sum(lp[0:1] * lp[1:2], axis=-1, keepdims=True))
           - jnp.exp(jnp.sum(lp[2:3] * lp[3:4], axis=-1, keepdims=True)) + lambda_init)
    subs = 2 * 2
    v_ext = [_values_with_ones(vt, hh) for hh in range(2)]

    def logits(i, c):
        j, sub = divmod(i, subs)
        qm = _masked_q(q_ref, j, DIFF_QK_DIM * sub, DIFF_QK_DIM * (sub + 1))
        return (lax.dot_general(k_ref[0, _k_rows(c), :], qm, _NT, preferred_element_type=F32)
                + _strip_window(strip_ref, sub // 2, j, c))

    parts = _attend(N_QBLK * subs, logits, lambda i: v_ext[(i % subs) // 2], bounded)
    for j in range(N_QBLK):
        outs = []
        for hh in range(2):
            comb = parts[j * subs + 2 * hh] - lam * parts[j * subs + 2 * hh + 1]
            ss = jnp.sum(comb * comb, axis=0, keepdims=True)
            outs.append(comb * lax.rsqrt(ss * (1.0 / DIFF_V_DIM) + EPS))
        y = jnp.concatenate(outs, axis=0).T * g_ref[...] * (1.0 - lambda_init)
        o_ref[0, _q_rows(j), :] = y.astype(BF16)


def _dil_kernel(q_ref, k_ref, vt_ref, strip_ref, o_ref, *, bounded):
    vt = vt_ref[0]
    v_ext = [_values_with_ones(vt, hh) for hh in range(2)]

    def logits(i, c):
        j, hh = divmod(i, 2)
        qm = _masked_q(q_ref, j, DIL_HEAD_DIM * hh, DIL_HEAD_DIM * (hh + 1))
        return (lax.dot_general(k_ref[0, _k_rows(c, DIL_TK), :], qm, _NT, preferred_element_type=F32)
                + _strip_window(strip_ref, hh, j, c, DIL_TK))

    outs = _attend(N_QBLK * 2, logits, lambda i: v_ext[i % 2], bounded, tk=DIL_TK,
                   needed=lambda i, c: _dil_chunk_needed(i // 2, c))
    for j in range(N_QBLK):
        o_ref[0, _q_rows(j), :] = jnp.concatenate(outs[2 * j:2 * j + 2], axis=0).T.astype(BF16)


def _mla_kernel(q_ref, k_ref, vt_ref, o_ref, *, bounded):
    vt = vt_ref[0]
    v_ext = [_values_with_ones(vt, hh) for hh in range(2)]

    def logits(i, c):
        j, hh = divmod(i, 2)
        return lax.dot_general(k_ref[0, _k_rows(c), hh * LANE:(hh + 1) * LANE],
                               q_ref[0, _q_rows(j), hh * LANE:(hh + 1) * LANE], _NT,
                               preferred_element_type=F32)

    outs = _attend(N_QBLK * 2, logits, lambda i: v_ext[i % 2], bounded)
    for j in range(N_QBLK):
        o_ref[0, _q_rows(j), :] = jnp.concatenate(outs[2 * j:2 * j + 2], axis=0).T.astype(BF16)


def _attention(kernel, name, bounded, q, k, vt, qk_lanes, strips=None, head0=0, extra=()):
    b, s, _ = q.shape
    n_pairs = vt.shape[1] // LANE
    in_specs = [
        pl.BlockSpec((1, s, qk_lanes), lambda p, bb: (bb, 0, p)),
        pl.BlockSpec((1, s, qk_lanes), lambda p, bb: (bb, 0, p)),
        pl.BlockSpec((1, LANE, s), lambda p, bb: (bb, p, 0)),
    ]
    args = [q, k, vt]
    if strips is not None:
        in_specs.append(pl.BlockSpec((2, STRIP_ROWS, ATT_TQ), lambda p, bb: (head0 // 2 + p, 0, 0),
                                     pipeline_mode=pl.Buffered(1)))
        args.append(strips)
    for a in extra:
        per_pair = a.shape[-1] == n_pairs * LANE
        in_specs.append(pl.BlockSpec((1, LANE), lambda p, bb: (0, p)) if per_pair
                        else pl.BlockSpec(a.shape, lambda p, bb: (0, 0)))
        args.append(a)

    return pl.pallas_call(
        functools.partial(kernel, bounded=bounded),
        name=name + ("_bounded" if bounded else "_online"),
        grid=(n_pairs, b),
        in_specs=in_specs,
        out_specs=pl.BlockSpec((1, s, LANE), lambda p, bb: (bb, 0, p)),
        out_shape=jax.ShapeDtypeStruct((b, s, n_pairs * LANE), BF16),
        compiler_params=_params("parallel", "parallel"),
    )(*args)


def _mixers(bounded, lambda_init):
    def run(dq, dk, dvt, lq, lk, lvt, mq, mk, mvt, strips, lam, subln):
        out_a = _attention(functools.partial(_diff_kernel, lambda_init=lambda_init), "diff_attn", bounded,
                           dq, dk, dvt, LANE, strips, 0, extra=(lam, subln))
        out_b = _attention(_dil_kernel, "dil_attn", bounded, lq, lk, lvt, LANE, strips, DIFF_HEADS)
        out_c = _attention(_mla_kernel, "mla_attn", bounded, mq, mk, mvt, 2 * LANE)
        return out_a, out_b, out_c
    return run


def _logit_bound(gain_q, gain_k, dim, bias_abs=0.0):
    return dim * jnp.max(jnp.abs(gain_q)) * jnp.max(jnp.abs(gain_k)) + bias_abs


def _rope_tables():
    half = MLA_ROPE_DIM // 2
    inv = ROPE_THETA ** (-jnp.arange(half, dtype=F32) / half)
    ang = jnp.arange(SEQ).astype(F32)[:, None] * inv[None, :]
    cos, sin = jnp.cos(ang), jnp.sin(ang)
    ones = jnp.ones((SEQ, MLA_NOPE_DIM), F32)
    tail = LANE - MLA_QK_DIM
    cos_t = jnp.concatenate([ones, cos, cos, jnp.ones((SEQ, tail), F32)], axis=1)
    sin_t = jnp.concatenate([0.0 * ones, -sin, sin, jnp.zeros((SEQ, tail), F32)], axis=1)
    return cos_t, sin_t


def _proj_weights(l, mix_norm, w_in, diff_q_norm, diff_k_norm, dil_q_norm, dil_k_norm,
                  mla_q_norm, mla_q_up, mla_kv_norm, mla_kv_up, mla_qn, mla_kn, cos_t, sin_t):
    wi = w_in[l]
    pad_to_block = LANE - MLA_QK_DIM
    w_ext = jnp.concatenate([wi[:, :SEG_ROPE], jnp.zeros((D_MODEL, MLA_NOPE_DIM), F32),
                             wi[:, SEG_ROPE:], jnp.zeros((D_MODEL, pad_to_block), F32)], axis=1)
    q_up = jnp.pad(mla_q_up[l].reshape(MLA_Q_RANK, MLA_HEADS, MLA_QK_DIM), ((0, 0), (0, 0), (0, pad_to_block)))
    kv = mla_kv_up[l].reshape(MLA_KV_RANK, MLA_HEADS, MLA_NOPE_DIM + MLA_V_DIM)
    kv_k = jnp.pad(kv[:, :, :MLA_NOPE_DIM], ((0, 0), (0, 0), (0, LANE - MLA_NOPE_DIM)))
    kv_v = kv[:, :, MLA_NOPE_DIM:]

    def head_gain(g, heads, scale=1.0, pad=0):
        return jnp.tile(jnp.pad(g.astype(F32) * scale, (0, pad)), heads)[None, :]

    return {
        "g": mix_norm[l][None, :],
        "w_in": w_ext.astype(BF16),
        "q_up": q_up.reshape(MLA_Q_RANK, MLA_PAD).astype(BF16),
        "kv_k": kv_k.reshape(MLA_KV_RANK, MLA_PAD).astype(BF16),
        "kv_v": kv_v.reshape(MLA_KV_RANK, MLA_WIDTH).astype(BF16),
        "gdq": head_gain(diff_q_norm[l], 2 * DIFF_HEADS, DIFF_QK_DIM ** -0.5 * LOG2E),
        "gdk": head_gain(diff_k_norm[l], 2 * DIFF_HEADS),
        "glq": head_gain(dil_q_norm[l], DIL_HEADS, DIL_HEAD_DIM ** -0.5 * LOG2E),
        "glk": head_gain(dil_k_norm[l], DIL_HEADS),
        "gmq": mla_q_norm[l][None, :],
        "gmkv": mla_kv_norm[l][None, :],
        "gqn": head_gain(mla_qn[l], MLA_HEADS, MLA_QK_DIM ** -0.5 * LOG2E, pad_to_block),
        "gkn": head_gain(mla_kn[l], MLA_HEADS, 1.0, pad_to_block),
        "cos": cos_t,
        "sin": sin_t,
    }


def kernel(x, rel_bias, ffn1_norm, ffn1_wg, ffn1_wu, ffn1_wd, mix_norm, w_in, diff_q_norm, diff_k_norm, diff_lambda, diff_subln, dil_q_norm, dil_k_norm, mla_q_norm, mla_q_up, mla_kv_norm, mla_kv_up, mla_qn, mla_kn, w_o, ffn2_norm, ffn2_wg, ffn2_wu, ffn2_wd):
    b, s, d = x.shape
    assert (s, d) == (SEQ, D_MODEL)
    t = b * s
    strips = _bias_strips(_strip_sources(rel_bias))
    cos_t, sin_t = _rope_tables()
    x2d = x.reshape(t, d)
    ffn1 = (ffn1_wg, ffn1_wu, ffn1_wd)
    ffn2 = (ffn2_wg, ffn2_wu, ffn2_wd)
    w_o_bf16 = _cast_stack(w_o)
    for l in range(DEPTH):
        lambda_init = 0.8 - 0.6 * math.exp(-0.3 * (l + 1))
        x2d = _ffn(x2d, l, ffn1_norm, *ffn1)
        pw = _proj_weights(l, mix_norm, w_in, diff_q_norm, diff_k_norm, dil_q_norm, dil_k_norm,
                           mla_q_norm, mla_q_up, mla_kv_norm, mla_kv_up, mla_qn, mla_kn, cos_t, sin_t)
        dq, dk, dvt, lq, lk, lvt, mq, mk, mvt = _proj(x2d, pw, b)
        tok = lambda a: a.reshape(b, s, -1)
        subln = jnp.tile(diff_subln[l].astype(F32), DIFF_HEADS)[None, :]
        bias_abs = jnp.max(jnp.abs(rel_bias.astype(F32)), axis=0) * LOG2E
        bound_a = _logit_bound(pw["gdq"], pw["gdk"], DIFF_QK_DIM, jnp.max(bias_abs[:DIFF_HEADS]))
        bound_b = _logit_bound(pw["glq"], pw["glk"], DIL_HEAD_DIM,
                               jnp.max(bias_abs[DIFF_HEADS:]) + math.log2(len(DIL_PATTERNS)))
        bound_c = _logit_bound(pw["gqn"], pw["gkn"], MLA_QK_DIM)
        logits_bounded = jnp.maximum(jnp.maximum(bound_a, bound_b), bound_c) <= LOGIT_BOUND
        out_a, out_b, out_c = lax.cond(
            logits_bounded, _mixers(True, lambda_init), _mixers(False, lambda_init),
            tok(dq), tok(dk), dvt, tok(lq), tok(lk), lvt, tok(mq), tok(mk), mvt, strips,
            diff_lambda[l].astype(F32), subln)
        mix = (out_a.reshape(t, -1), out_b.reshape(t, -1), out_c.reshape(t, -1), w_o_bf16)
        x2d = _ffn(x2d, l, ffn2_norm, *ffn2, mix=mix)
    return x2d.reshape(b, s, d)
```

```python
import functools
import math

import jax
import jax.numpy as jnp
import numpy as np
from jax import lax
from jax.experimental import pallas as pl
from jax.experimental.pallas import tpu as pltpu

F32 = jnp.float32
BF16 = jnp.bfloat16

D_MODEL = 1024
SEQ = 2048
DEPTH = 2
EPS = 1e-6
D_FF = 2816
NUM_BUCKETS = 32
MAX_DISTANCE = 1024
ROPE_THETA = 10000.0
NEG_BIG = -1e30
LOG2E = math.log2(math.e)
DIFF_HEADS, DIFF_QK_DIM, DIFF_V_DIM = 4, 32, 64
DIL_HEADS, DIL_HEAD_DIM = 6, 64
DIL_PATTERNS = ((128, 1), (512, 4), (2048, 16))
MLA_HEADS, MLA_Q_RANK, MLA_KV_RANK = 6, 256, 128
MLA_NOPE_DIM, MLA_ROPE_DIM, MLA_V_DIM = 64, 32, 64
MLA_QK_DIM = MLA_NOPE_DIM + MLA_ROPE_DIM
HEAD_V = 64
DIFF_WIDTH = DIFF_HEADS * DIFF_V_DIM
DIL_WIDTH = DIL_HEADS * DIL_HEAD_DIM
MLA_WIDTH = MLA_HEADS * MLA_V_DIM
BIAS_HEADS = DIFF_HEADS + DIL_HEADS

LANE = 128
SUBLANE = 8
BF16_ROWS = 16
MXU_DIM = 256
VMEM_LIMIT_BYTES = 56 * 1024 * 1024

FFN_TM = 512
CAST_BLOCKS = 4
STAGE_CHUNKS = 8
PROJ_TM = 512
PROJ_SPLIT = 2
ATT_TQ = 512
ATT_TK = 2048
DIL_TK = 512
MAX_SLAB = 128
LOGIT_BOUND = 50.0
STRIP_ROWS = 2 * SEQ - ATT_TQ
STRIP_SRC_W = 2 * SEQ + 2 * LANE
N_QBLK = SEQ // ATT_TQ

DIFF_QK_WIDTH = DIFF_HEADS * 2 * DIFF_QK_DIM
SEG_DQ = 0
SEG_DK = SEG_DQ + DIFF_QK_WIDTH
SEG_DV = SEG_DK + DIFF_QK_WIDTH
SEG_LQ = SEG_DV + DIFF_WIDTH
SEG_LK = SEG_LQ + DIL_WIDTH
SEG_LV = SEG_LK + DIL_WIDTH
SEG_MQ = SEG_LV + DIL_WIDTH
SEG_MKV = SEG_MQ + MLA_Q_RANK
SEG_ROPE = SEG_MKV + MLA_KV_RANK
IN_EXT = SEG_ROPE + LANE
MLA_PAD = MLA_HEADS * LANE


def _params(*sem):
    return pltpu.CompilerParams(dimension_semantics=sem, vmem_limit_bytes=VMEM_LIMIT_BYTES)


def _rms(x, g):
    return x * lax.rsqrt(jnp.mean(x * x, axis=-1, keepdims=True) + EPS) * g


def _cast_kernel(w_ref, o_ref):
    o_ref[...] = w_ref[...].astype(BF16)


def _cast_stack(w):
    layers, rows, cols = w.shape
    blk = pl.BlockSpec((1, rows // CAST_BLOCKS, cols), lambda l, r: (l, r, 0))
    return pl.pallas_call(
        _cast_kernel,
        name="cast_bf16",
        grid=(layers, CAST_BLOCKS),
        in_specs=[blk],
        out_specs=blk,
        out_shape=jax.ShapeDtypeStruct(w.shape, BF16),
        compiler_params=_params("parallel", "parallel"),
    )(w)


def _stage_weight(w_hbm, layer, w_scr, stage, sem):
    rows = w_scr.shape[0]
    chunk = stage.shape[1]
    n = rows // chunk

    def copy(c):
        return pltpu.make_async_copy(w_hbm.at[layer, pl.ds(c * chunk, chunk)], stage.at[c % 2], sem.at[c % 2])

    copy(0).start()
    for c in range(n):
        if c + 1 < n:
            copy(c + 1).start()
        copy(c).wait()
        w_scr[c * chunk:(c + 1) * chunk, :] = stage[c % 2].astype(BF16)


def _ffn_kernel(*refs, fused_out, layer):
    if fused_out:
        x_ref, a_ref, b_ref, c_ref, wo_ref, g_ref, wg_hbm, wu_hbm, wd_hbm, o_ref = refs[:10]
        mix_refs = (a_ref, b_ref, c_ref)
    else:
        x_ref, g_ref, wg_hbm, wu_hbm, wd_hbm, o_ref = refs[:6]
        mix_refs = ()
    wg_ref, wu_ref, wd_ref, stage_in, stage_out, sem = refs[-6:]

    @pl.when(pl.program_id(0) == 0)
    def _():
        _stage_weight(wg_hbm, layer, wg_ref, stage_in, sem)
        _stage_weight(wu_hbm, layer, wu_ref, stage_in, sem)
        _stage_weight(wd_hbm, layer, wd_ref, stage_out, sem)

    x = x_ref[...]
    row = 0
    for part in mix_refs:
        width = part.shape[1]
        x += jnp.dot(part[...], wo_ref[row:row + width, :], preferred_element_type=F32)
        row += width
    n = _rms(x, g_ref[...]).astype(BF16)
    gate = jnp.dot(n, wg_ref[...], preferred_element_type=F32)
    up = jnp.dot(n, wu_ref[...], preferred_element_type=F32)
    h = (gate * jax.nn.sigmoid(gate) * up).astype(BF16)
    o_ref[...] = x + 0.5 * jnp.dot(h, wd_ref[...], preferred_element_type=F32)


def _ffn(x2d, layer, g, wg, wu, wd, mix=None):
    t = x2d.shape[0]
    tm = FFN_TM
    row = lambda i: (i, 0)

    def resident(a):
        return pl.BlockSpec((None,) + a.shape[1:], lambda i: (layer, 0, 0), pipeline_mode=pl.Buffered(1))

    args, in_specs = [x2d], [pl.BlockSpec((tm, D_MODEL), row)]
    if mix is not None:
        a, b, c, wo = mix
        args += [a, b, c, wo]
        in_specs += [pl.BlockSpec((tm, m.shape[1]), row) for m in (a, b, c)] + [resident(wo)]
    g3 = g[:, None, :]
    hbm = pl.BlockSpec(memory_space=pl.ANY)
    return pl.pallas_call(
        functools.partial(_ffn_kernel, fused_out=mix is not None, layer=layer),
        name="ffn_out" if mix is not None else "ffn",
        grid=(t // tm,),
        in_specs=in_specs + [resident(g3), hbm, hbm, hbm],
        out_specs=pl.BlockSpec((tm, D_MODEL), row),
        out_shape=jax.ShapeDtypeStruct((t, D_MODEL), F32),
        scratch_shapes=[
            pltpu.VMEM((D_MODEL, D_FF), BF16), pltpu.VMEM((D_MODEL, D_FF), BF16), pltpu.VMEM((D_FF, D_MODEL), BF16),
            pltpu.VMEM((2, D_MODEL // STAGE_CHUNKS, D_FF), F32), pltpu.VMEM((2, D_FF // STAGE_CHUNKS, D_MODEL), F32),
            pltpu.SemaphoreType.DMA((2,)),
        ],
        compiler_params=_params("arbitrary"),
    )(*args, g3, wg, wu, wd)


def _group_norm(x, gsum, inv_n, gain):
    ss = jnp.dot((x * x).astype(BF16), gsum, preferred_element_type=F32)
    return x * lax.rsqrt(ss * inv_n + EPS) * gain


def _group_norm_chunks(x, gsum, inv_n, gain):
    w = gsum.shape[0]
    parts = [_group_norm(x[:, c:c + w], gsum, inv_n, gain[:, c:c + w]) for c in range(0, x.shape[1], w)]
    return jnp.concatenate(parts, axis=1)


def _rope_blocks(x, cos, sin):
    lane = lax.broadcasted_iota(jnp.int32, (x.shape[0], LANE), 1)
    first_half = lane < MLA_NOPE_DIM + MLA_ROPE_DIM // 2
    parts = []
    for c in range(0, x.shape[1], LANE):
        xb = x[:, c:c + LANE]
        partner = jnp.where(first_half,
                            pltpu.roll(xb, LANE - MLA_ROPE_DIM // 2, 1),
                            pltpu.roll(xb, MLA_ROPE_DIM // 2, 1))
        parts.append(xb * cos + partner * sin)
    return jnp.concatenate(parts, axis=1)


def _proj_kernel(x_ref, g_ref, win_ref, qup_ref, kvk_ref, kvv_ref,
                 gdq_ref, gdk_ref, glq_ref, glk_ref, gmq_ref, gmkv_ref, gqn_ref, gkn_ref,
                 cos_ref, sin_ref, g32_ref, g64_ref, g128_ref,
                 dq_ref, dk_ref, dvt_ref, lq_ref, lk_ref, lvt_ref, mq_ref, mk_ref, mvt_ref):
    g32, g64, g128 = g32_ref[...], g64_ref[...], g128_ref[...]
    sub = PROJ_TM // PROJ_SPLIT
    for r in range(PROJ_SPLIT):
        rows = slice(r * sub, (r + 1) * sub)
        h = _rms(x_ref[rows, :], g_ref[...]).astype(BF16)
        proj = jnp.dot(h, win_ref[...], preferred_element_type=F32)

        def seg(lo, hi):
            return proj[:, lo:hi]

        dq_ref[rows, :] = _group_norm(seg(SEG_DQ, SEG_DK), g32, 1.0 / DIFF_QK_DIM, gdq_ref[...]).astype(BF16)
        dk_ref[rows, :] = _group_norm(seg(SEG_DK, SEG_DV), g32, 1.0 / DIFF_QK_DIM, gdk_ref[...]).astype(BF16)
        dvt_ref[0, :, rows] = seg(SEG_DV, SEG_LQ).T.astype(BF16)
        lqk = _group_norm_chunks(seg(SEG_LQ, SEG_LV), g64, 1.0 / DIL_HEAD_DIM,
                                 jnp.concatenate([glq_ref[...], glk_ref[...]], axis=1))
        lq_ref[rows, :] = lqk[:, :DIL_WIDTH].astype(BF16)
        lk_ref[rows, :] = lqk[:, DIL_WIDTH:].astype(BF16)
        lvt_ref[0, :, rows] = seg(SEG_LV, SEG_MQ).T.astype(BF16)

        cos, sin = cos_ref[rows, :], sin_ref[rows, :]
        q_lat = _rms(seg(SEG_MQ, SEG_MKV), gmq_ref[...]).astype(BF16)
        q = jnp.dot(q_lat, qup_ref[...], preferred_element_type=F32)
        q = _group_norm_chunks(q, g128, 1.0 / MLA_QK_DIM, gqn_ref[...])
        mq_ref[rows, :] = _rope_blocks(q, cos, sin).astype(BF16)

        c_kv = _rms(seg(SEG_MKV, SEG_ROPE), gmkv_ref[...]).astype(BF16)
        k_rope = seg(SEG_ROPE, IN_EXT)
        k = jnp.dot(c_kv, kvk_ref[...], preferred_element_type=F32)
        k = k + jnp.concatenate([k_rope] * MLA_HEADS, axis=1)
        k = _group_norm_chunks(k, g128, 1.0 / MLA_QK_DIM, gkn_ref[...])
        mk_ref[rows, :] = _rope_blocks(k, cos, sin).astype(BF16)
        mvt_ref[0, :, rows] = jnp.dot(c_kv, kvv_ref[...], preferred_element_type=F32).T.astype(BF16)


def _block_diag_ones(width, group):
    idx = np.arange(width) // group
    return jnp.asarray((idx[:, None] == idx[None, :]).astype(np.float32), dtype=BF16)


def _proj(x2d, pw, batch):
    t = x2d.shape[0]
    tm = PROJ_TM
    row = lambda i: (i, 0)
    fixed = lambda i: (0, 0)
    pos_blocks = SEQ // tm
    col = lambda i: (i // pos_blocks, 0, i % pos_blocks)

    def full(a):
        return pl.BlockSpec(a.shape, fixed)

    consts = [pw["g"], pw["w_in"], pw["q_up"], pw["kv_k"], pw["kv_v"],
              pw["gdq"], pw["gdk"], pw["glq"], pw["glk"], pw["gmq"], pw["gmkv"], pw["gqn"], pw["gkn"]]
    tables = [pw["cos"], pw["sin"]]
    gsums = [_block_diag_ones(MXU_DIM, DIFF_QK_DIM), _block_diag_ones(MXU_DIM, DIL_HEAD_DIM),
             _block_diag_ones(MXU_DIM, LANE)]
    outs = [(DIFF_QK_WIDTH, False), (DIFF_QK_WIDTH, False), (DIFF_WIDTH, True), (DIL_WIDTH, False), (DIL_WIDTH, False),
            (DIL_WIDTH, True), (MLA_PAD, False), (MLA_PAD, False), (MLA_WIDTH, True)]
    return pl.pallas_call(
        _proj_kernel,
        name="proj",
        grid=(t // tm,),
        in_specs=([pl.BlockSpec((tm, D_MODEL), row)] + [full(a) for a in consts]
                  + [pl.BlockSpec((tm, LANE), lambda i: (i % pos_blocks, 0)) for _ in tables]
                  + [full(a) for a in gsums]),
        out_specs=[pl.BlockSpec((1, w, tm), col) if tr else pl.BlockSpec((tm, w), row) for w, tr in outs],
        out_shape=[jax.ShapeDtypeStruct((batch, w, SEQ) if tr else (t, w), BF16) for w, tr in outs],
        compiler_params=_params("parallel"),
    )(x2d, *consts, *tables, *gsums)


def _strip_kernel(rev_ref, o_ref, src_scr):
    row = rev_ref[0]
    for s in range(SUBLANE):
        shift = SUBLANE - 1 - s
        src_scr[s:s + 1, :] = pltpu.roll(row, STRIP_SRC_W - shift, 1) if shift else row
    chunk_w = ATT_TQ + LANE
    for a in range(STRIP_ROWS // SUBLANE):
        start = STRIP_ROWS - SUBLANE - SUBLANE * a
        lo = (start // LANE) * LANE
        chunk = src_scr[:, lo:lo + chunk_w]
        shift = start - lo
        if shift:
            chunk = pltpu.roll(chunk, chunk_w - shift, 1)
        o_ref[0, a * SUBLANE:(a + 1) * SUBLANE, :] = chunk[:, :ATT_TQ]


def _bias_strips(rev):
    nh = rev.shape[0]
    return pl.pallas_call(
        _strip_kernel,
        name="bias_strips",
        grid=(nh,),
        in_specs=[pl.BlockSpec((1, 1, STRIP_SRC_W), lambda h: (h, 0, 0))],
        out_specs=pl.BlockSpec((1, STRIP_ROWS, ATT_TQ), lambda h: (h, 0, 0)),
        out_shape=jax.ShapeDtypeStruct((nh, STRIP_ROWS, ATT_TQ), F32),
        scratch_shapes=[pltpu.VMEM((SUBLANE, STRIP_SRC_W), F32)],
        compiler_params=_params("parallel"),
    )(rev)


def _t5_bucket(rel):
    half = NUM_BUCKETS // 2
    max_exact = half // 2
    n = jnp.abs(rel)
    nf = jnp.maximum(n, 1).astype(F32)
    large = max_exact + (jnp.log(nf / max_exact) / math.log(MAX_DISTANCE / max_exact)
                         * (half - max_exact)).astype(jnp.int32)
    large = jnp.minimum(large, half - 1)
    return jnp.where(rel > 0, half, 0) + jnp.where(n < max_exact, n, large)


def _strip_sources(rel_bias):
    rel = np.arange(-(SEQ - 1), SEQ)
    mult = np.zeros(rel.shape, np.int64)
    for window, dil in DIL_PATTERNS:
        mult += (rel % dil == 0) & (np.abs(rel) // dil <= window // (2 * dil))
    bucket = _t5_bucket(jnp.asarray(rel, jnp.int32))
    table = rel_bias.astype(F32)[bucket].T
    is_dil = jnp.asarray(np.arange(BIAS_HEADS) >= DIFF_HEADS)[:, None]
    logm = jnp.asarray(np.log(np.maximum(mult, 1)), F32)[None, :]
    reachable = jnp.asarray(mult > 0)[None, :]
    vec = jnp.where(is_dil & ~reachable, NEG_BIG, table + jnp.where(is_dil, logm, 0.0)) * LOG2E
    rev = jnp.pad(vec[:, ::-1], ((0, 0), (0, STRIP_SRC_W - vec.shape[1])))
    return rev[:, None, :]


_NT = (((1,), (1,)), ((), ()))


def _lanes_between(shape, lo, hi):
    lane = lax.broadcasted_iota(jnp.int32, shape, 1)
    return (lane >= lo) & (lane < hi)


def _values_with_ones(vt, hh):
    return jnp.concatenate([vt[hh * HEAD_V:(hh + 1) * HEAD_V], jnp.ones((BF16_ROWS, vt.shape[1]), BF16)], axis=0)


def _col_max(s_t):
    part = jnp.max(s_t.reshape(s_t.shape[0] // MAX_SLAB, MAX_SLAB, s_t.shape[1]), axis=0)
    return jnp.max(part, axis=0, keepdims=True)


def _attend(n_items, logits_fn, values_fn, bounded, tk=ATT_TK, needed=lambda item, chunk: True):
    stages = [(i, c) for i in range(n_items) for c in range(SEQ // tk) if needed(i, c)]
    outs = []
    s_next = logits_fn(*stages[0])
    m = acc = None
    for idx, (item, c) in enumerate(stages):
        first = idx == 0 or stages[idx - 1][0] != item
        last = idx + 1 == len(stages) or stages[idx + 1][0] != item
        s_cur = s_next
        if idx + 1 < len(stages):
            s_next = logits_fn(*stages[idx + 1])
        values = values_fn(item)[:, c * tk:(c + 1) * tk]
        if bounded:
            pv = jnp.dot(values, jnp.exp2(s_cur).astype(BF16), preferred_element_type=F32)
            acc = pv if first else acc + pv
        else:
            c_max = _col_max(s_cur)
            m_new = c_max if first else jnp.maximum(m, c_max)
            pv = jnp.dot(values, jnp.exp2((s_cur - m_new).astype(BF16)), preferred_element_type=F32)
            acc = pv if first else acc * jnp.exp2(m - m_new) + pv
            m = m_new
        if last:
            outs.append(acc[:HEAD_V] * (1.0 / acc[HEAD_V:HEAD_V + 1]))
    return outs


def _q_rows(j):
    return slice(j * ATT_TQ, (j + 1) * ATT_TQ)


def _k_rows(c, tk=ATT_TK):
    return slice(c * tk, (c + 1) * tk)


def _strip_window(strip_ref, hh, qblk, c, tk=ATT_TK):
    off = (N_QBLK - 1 - qblk) * ATT_TQ + c * tk
    return strip_ref[hh, off:off + tk, :]


def _dil_chunk_needed(qblk, c):
    reach = max(window // 2 for window, _ in DIL_PATTERNS)
    q_lo, q_hi = qblk * ATT_TQ, (qblk + 1) * ATT_TQ - 1
    k_lo, k_hi = c * DIL_TK, (c + 1) * DIL_TK - 1
    return max(k_lo - q_hi, q_lo - k_hi, 0) <= reach


def _masked_q(q_ref, j, lo, hi):
    q = q_ref[0, _q_rows(j), :]
    return jnp.where(_lanes_between(q.shape, lo, hi), q, jnp.zeros_like(q))


def _diff_kernel(q_ref, k_ref, vt_ref, strip_ref, lam_ref, g_ref, o_ref, *, lambda_init, bounded):
    vt = vt_ref[0]
    lp = lam_ref[...]
    lam = (jnp.exp(jnp.sum(lp[0:1] * lp[1:2], axis=-1, keepdims=True))
           - jnp.exp(jnp.sum(lp[2:3] * lp[3:4], axis=-1, keepdims=True)) + lambda_init)
    subs = 2 * 2
    v_ext = [_values_with_ones(vt, hh) for hh in range(2)]

    def logits(i, c):
        j, sub = divmod(i, subs)
        qm = _masked_q(q_ref, j, DIFF_QK_DIM * sub, DIFF_QK_DIM * (sub + 1))
        return (lax.dot_general(k_ref[0, _k_rows(c), :], qm, _NT, preferred_element_type=F32)
                + _strip_window(strip_ref, sub // 2, j, c))

    parts = _attend(N_QBLK * subs, logits, lambda i: v_ext[(i % subs) // 2], bounded)
    for j in range(N_QBLK):
        outs = []
        for hh in range(2):
            comb = parts[j * subs + 2 * hh] - lam * parts[j * subs + 2 * hh + 1]
            ss = jnp.sum(comb * comb, axis=0, keepdims=True)
            outs.append(comb * lax.rsqrt(ss * (1.0 / DIFF_V_DIM) + EPS))
        y = jnp.concatenate(outs, axis=0).T * g_ref[...] * (1.0 - lambda_init)
        o_ref[0, _q_rows(j), :] = y.astype(BF16)


def _dil_kernel(q_ref, k_ref, vt_ref, strip_ref, o_ref, *, bounded):
    vt = vt_ref[0]
    v_ext = [_values_with_ones(vt, hh) for hh in range(2)]

    def logits(i, c):
        j, hh = divmod(i, 2)
        qm = _masked_q(q_ref, j, DIL_HEAD_DIM * hh, DIL_HEAD_DIM * (hh + 1))
        return (lax.dot_general(k_ref[0, _k_rows(c, DIL_TK), :], qm, _NT, preferred_element_type=F32)
                + _strip_window(strip_ref, hh, j, c, DIL_TK))

    outs = _attend(N_QBLK * 2, logits, lambda i: v_ext[i % 2], bounded, tk=DIL_TK,
                   needed=lambda i, c: _dil_chunk_needed(i // 2, c))
    for j in range(N_QBLK):
        o_ref[0, _q_rows(j), :] = jnp.concatenate(outs[2 * j:2 * j + 2], axis=0).T.astype(BF16)


def _mla_kernel(q_ref, k_ref, vt_ref, o_ref, *, bounded):
    vt = vt_ref[0]
    v_ext = [_values_with_ones(vt, hh) for hh in range(2)]

    def logits(i, c):
        j, hh = divmod(i, 2)
        return lax.dot_general(k_ref[0, _k_rows(c), hh * LANE:(hh + 1) * LANE],
                               q_ref[0, _q_rows(j), hh * LANE:(hh + 1) * LANE], _NT,
                               preferred_element_type=F32)

    outs = _attend(N_QBLK * 2, logits, lambda i: v_ext[i % 2], bounded)
    for j in range(N_QBLK):
        o_ref[0, _q_rows(j), :] = jnp.concatenate(outs[2 * j:2 * j + 2], axis=0).T.astype(BF16)


def _attention(kernel, name, bounded, q, k, vt, qk_lanes, strips=None, head0=0, extra=()):
    b, s, _ = q.shape
    n_pairs = vt.shape[1] // LANE
    in_specs = [
        pl.BlockSpec((1, s, qk_lanes), lambda p, bb: (bb, 0, p)),
        pl.BlockSpec((1, s, qk_lanes), lambda p, bb: (bb, 0, p)),
        pl.BlockSpec((1, LANE, s), lambda p, bb: (bb, p, 0)),
    ]
    args = [q, k, vt]
    if strips is not None:
        in_specs.append(pl.BlockSpec((2, STRIP_ROWS, ATT_TQ), lambda p, bb: (head0 // 2 + p, 0, 0),
                                     pipeline_mode=pl.Buffered(1)))
        args.append(strips)
    for a in extra:
        per_pair = a.shape[-1] == n_pairs * LANE
        in_specs.append(pl.BlockSpec((1, LANE), lambda p, bb: (0, p)) if per_pair
                        else pl.BlockSpec(a.shape, lambda p, bb: (0, 0)))
        args.append(a)

    return pl.pallas_call(
        functools.partial(kernel, bounded=bounded),
        name=name + ("_bounded" if bounded else "_online"),
        grid=(n_pairs, b),
        in_specs=in_specs,
        out_specs=pl.BlockSpec((1, s, LANE), lambda p, bb: (bb, 0, p)),
        out_shape=jax.ShapeDtypeStruct((b, s, n_pairs * LANE), BF16),
        compiler_params=_params("parallel", "parallel"),
    )(*args)


def _mixers(bounded, lambda_init):
    def run(dq, dk, dvt, lq, lk, lvt, mq, mk, mvt, strips, lam, subln):
        out_a = _attention(functools.partial(_diff_kernel, lambda_init=lambda_init), "diff_attn", bounded,
                           dq, dk, dvt, LANE, strips, 0, extra=(lam, subln))
        out_b = _attention(_dil_kernel, "dil_attn", bounded, lq, lk, lvt, LANE, strips, DIFF_HEADS)
        out_c = _attention(_mla_kernel, "mla_attn", bounded, mq, mk, mvt, 2 * LANE)
        return out_a, out_b, out_c
    return run


def _logit_bound(gain_q, gain_k, dim, bias_abs=0.0):
    return dim * jnp.max(jnp.abs(gain_q)) * jnp.max(jnp.abs(gain_k)) + bias_abs


def _rope_tables():
    half = MLA_ROPE_DIM // 2
    inv = ROPE_THETA ** (-jnp.arange(half, dtype=F32) / half)
    ang = jnp.arange(SEQ).astype(F32)[:, None] * inv[None, :]
    cos, sin = jnp.cos(ang), jnp.sin(ang)
    ones = jnp.ones((SEQ, MLA_NOPE_DIM), F32)
    tail = LANE - MLA_QK_DIM
    cos_t = jnp.concatenate([ones, cos, cos, jnp.ones((SEQ, tail), F32)], axis=1)
    sin_t = jnp.concatenate([0.0 * ones, -sin, sin, jnp.zeros((SEQ, tail), F32)], axis=1)
    return cos_t, sin_t


def _proj_weights(l, mix_norm, w_in, diff_q_norm, diff_k_norm, dil_q_norm, dil_k_norm,
                  mla_q_norm, mla_q_up, mla_kv_norm, mla_kv_up, mla_qn, mla_kn, cos_t, sin_t):
    wi = w_in[l]
    pad_to_block = LANE - MLA_QK_DIM
    w_ext = jnp.concatenate([wi[:, :SEG_ROPE], jnp.zeros((D_MODEL, MLA_NOPE_DIM), F32),
                             wi[:, SEG_ROPE:], jnp.zeros((D_MODEL, pad_to_block), F32)], axis=1)
    q_up = jnp.pad(mla_q_up[l].reshape(MLA_Q_RANK, MLA_HEADS, MLA_QK_DIM), ((0, 0), (0, 0), (0, pad_to_block)))
    kv = mla_kv_up[l].reshape(MLA_KV_RANK, MLA_HEADS, MLA_NOPE_DIM + MLA_V_DIM)
    kv_k = jnp.pad(kv[:, :, :MLA_NOPE_DIM], ((0, 0), (0, 0), (0, LANE - MLA_NOPE_DIM)))
    kv_v = kv[:, :, MLA_NOPE_DIM:]

    def head_gain(g, heads, scale=1.0, pad=0):
        return jnp.tile(jnp.pad(g.astype(F32) * scale, (0, pad)), heads)[None, :]

    return {
        "g": mix_norm[l][None, :],
        "w_in": w_ext.astype(BF16),
        "q_up": q_up.reshape(MLA_Q_RANK, MLA_PAD).astype(BF16),
        "kv_k": kv_k.reshape(MLA_KV_RANK, MLA_PAD).astype(BF16),
        "kv_v": kv_v.reshape(MLA_KV_RANK, MLA_WIDTH).astype(BF16),
        "gdq": head_gain(diff_q_norm[l], 2 * DIFF_HEADS, DIFF_QK_DIM ** -0.5 * LOG2E),
        "gdk": head_gain(diff_k_norm[l], 2 * DIFF_HEADS),
        "glq": head_gain(dil_q_norm[l], DIL_HEADS, DIL_HEAD_DIM ** -0.5 * LOG2E),
        "glk": head_gain(dil_k_norm[l], DIL_HEADS),
        "gmq": mla_q_norm[l][None, :],
        "gmkv": mla_kv_norm[l][None, :],
        "gqn": head_gain(mla_qn[l], MLA_HEADS, MLA_QK_DIM ** -0.5 * LOG2E, pad_to_block),
        "gkn": head_gain(mla_kn[l], MLA_HEADS, 1.0, pad_to_block),
        "cos": cos_t,
        "sin": sin_t,
    }


def kernel(x, rel_bias, ffn1_norm, ffn1_wg, ffn1_wu, ffn1_wd, mix_norm, w_in, diff_q_norm, diff_k_norm, diff_lambda, diff_subln, dil_q_norm, dil_k_norm, mla_q_norm, mla_q_up, mla_kv_norm, mla_kv_up, mla_qn, mla_kn, w_o, ffn2_norm, ffn2_wg, ffn2_wu, ffn2_wd):
    b, s, d = x.shape
    assert (s, d) == (SEQ, D_MODEL)
    t = b * s
    strips = _bias_strips(_strip_sources(rel_bias))
    cos_t, sin_t = _rope_tables()
    x2d = x.reshape(t, d)
    ffn1 = (ffn1_wg, ffn1_wu, ffn1_wd)
    ffn2 = (ffn2_wg, ffn2_wu, ffn2_wd)
    w_o_bf16 = _cast_stack(w_o)
    for l in range(DEPTH):
        lambda_init = 0.8 - 0.6 * math.exp(-0.3 * (l + 1))
        x2d = _ffn(x2d, l, ffn1_norm, *ffn1)
        pw = _proj_weights(l, mix_norm, w_in, diff_q_norm, diff_k_norm, dil_q_norm, dil_k_norm,
                           mla_q_norm, mla_q_up, mla_kv_norm, mla_kv_up, mla_qn, mla_kn, cos_t, sin_t)
        dq, dk, dvt, lq, lk, lvt, mq, mk, mvt = _proj(x2d, pw, b)
        tok = lambda a: a.reshape(b, s, -1)
        subln = jnp.tile(diff_subln[l].astype(F32), DIFF_HEADS)[None, :]
        bias_abs = jnp.max(jnp.abs(rel_bias.astype(F32)), axis=0) * LOG2E
        bound_a = _logit_bound(pw["gdq"], pw["gdk"], DIFF_QK_DIM, jnp.max(bias_abs[:DIFF_HEADS]))
        bound_b = _logit_bound(pw["glq"], pw["glk"], DIL_HEAD_DIM,
                               jnp.max(bias_abs[DIFF_HEADS:]) + math.log2(len(DIL_PATTERNS)))
        bound_c = _logit_bound(pw["gqn"], pw["gkn"], MLA_QK_DIM)
        logits_bounded = jnp.maximum(jnp.maximum(bound_a, bound_b), bound_c) <= LOGIT_BOUND
        out_a, out_b, out_c = lax.cond(
            logits_bounded, _mixers(True, lambda_init), _mixers(False, lambda_init),
            tok(dq), tok(dk), dvt, tok(lq), tok(lk), lvt, tok(mq), tok(mk), mvt, strips,
            diff_lambda[l].astype(F32), subln)
        mix = (out_a.reshape(t, -1), out_b.reshape(t, -1), out_c.reshape(t, -1), w_o_bf16)
        x2d = _ffn(x2d, l, ffn2_norm, *ffn2, mix=mix)
    return x2d.reshape(b, s, d)
```
